```python
import jax, jax.numpy as jnp
from jax import lax
import numpy as np

D_MODEL = 1024
BATCH = 8
SEQ = 2048
DEPTH = 1
DEC_BATCH = 128
DEC_SEQ = 8
PAST_LEN = 16384
PAGE_SIZE = 128

MIX_WIDTH = D_MODEL
SGU_WIDTH = MIX_WIDTH // 2
CONV_WIDTH = MIX_WIDTH - SGU_WIDTH
SGU_HEADS = 4
SGU_HEAD_DIM = SGU_WIDTH // SGU_HEADS
CHUNK = 128
CONV_K = 31
MEM_TOKENS = 256
X_HEADS = 4
X_HEAD_DIM = D_MODEL // X_HEADS
FFN_DIM = 4 * D_MODEL
IN_COLS = 2 * SGU_WIDTH + 2 * CONV_WIDTH
EPS = 1e-6

kernel_name = 'hymba_sgu_conformer_macaron_decoder_step'


def rms_norm(x, g):
    xf = x.astype(jnp.float32)
    y = xf * lax.rsqrt(jnp.mean(xf * xf, axis=-1, keepdims=True) + EPS)
    return (y * g.astype(jnp.float32)).astype(x.dtype)


def layer_norm(x, g, b):
    xf = x.astype(jnp.float32)
    mu = jnp.mean(xf, axis=-1, keepdims=True)
    xc = xf - mu
    y = xc * lax.rsqrt(jnp.mean(xc * xc, axis=-1, keepdims=True) + EPS)
    return (y * g.astype(jnp.float32) + b.astype(jnp.float32)).astype(x.dtype)


def macaron_ffn(x, g_pre, w_gate, w_up, w_down, g_post):
    n = rms_norm(x, g_pre)
    f = (jax.nn.silu(n @ w_gate) * (n @ w_up)) @ w_down
    return x + 0.5 * rms_norm(f, g_post)


def spatial_gate(u, v, w_s, b_s):
    bsz, L, _ = v.shape
    n_chunks = -(-L // CHUNK)
    pad = n_chunks * CHUNK - L
    vp = jnp.pad(v, ((0, 0), (0, pad), (0, 0))).reshape(bsz, n_chunks, CHUNK, SGU_HEADS, SGU_HEAD_DIM)
    causal = jnp.tril(jnp.ones((CHUNK, CHUNK), dtype=bool))
    ws = jnp.where(causal, w_s, jnp.zeros_like(w_s))
    s = jnp.einsum('hij,bnjhd->bnihd', ws, vp) + b_s.T[:, :, None]
    s = s.reshape(bsz, n_chunks * CHUNK, SGU_WIDTH)[:, :L]
    return u * s


def causal_dwconv(a, left, w, b):
    full = jnp.concatenate([left, a], axis=1)
    y = lax.conv_general_dilated(full, w[:, None, :], window_strides=(1,), padding='VALID',
                                 dimension_numbers=('NWC', 'WIO', 'NWC'),
                                 feature_group_count=a.shape[-1])
    return y + b, full[:, -(CONV_K - 1):]


def token_mixing(h, conv_left, g_pre, w_in, sgu_ln_g, sgu_ln_b, w_s, b_s,
                 conv_w, conv_b, conv_ln_g, conv_ln_b, w_out, g_post):
    n = rms_norm(h, g_pre)
    z = n @ w_in
    u, v, a, gate = jnp.split(z, [SGU_WIDTH, 2 * SGU_WIDTH, 2 * SGU_WIDTH + CONV_WIDTH], axis=-1)
    v = layer_norm(v, sgu_ln_g, sgu_ln_b)
    out_a = spatial_gate(u, v, w_s, b_s)
    glu = a * jax.nn.sigmoid(gate)
    c, conv_state = causal_dwconv(glu, conv_left, conv_w, conv_b)
    out_b = jax.nn.silu(layer_norm(c, conv_ln_g, conv_ln_b))
    o = jnp.concatenate([out_a, out_b], axis=-1) @ w_out
    L = v.shape[1]
    rows = L - ((L - 1) // CHUNK) * CHUNK
    return h + rms_norm(o, g_post), conv_state, v[:, L - rows:]


def memory_kv(mem, g_mem, w_k, w_v):
    bsz, m, _ = mem.shape
    mn = rms_norm(mem, g_mem)
    k = (mn @ w_k).reshape(bsz, m, X_HEADS, X_HEAD_DIM)
    v = (mn @ w_v).reshape(bsz, m, X_HEADS, X_HEAD_DIM)
    return k, v


def cross_attend(h, k, v, g_pre, w_q, w_o, g_post):
    bsz, L, _ = h.shape
    n = rms_norm(h, g_pre)
    q = (n @ w_q).reshape(bsz, L, X_HEADS, X_HEAD_DIM)
    s = jnp.einsum('blhd,bmhd->bhlm', q, k).astype(jnp.float32) * (X_HEAD_DIM ** -0.5)
    p = jax.nn.softmax(s, axis=-1).astype(v.dtype)
    o = jnp.einsum('bhlm,bmhd->blhd', p, v).reshape(bsz, L, X_HEADS * X_HEAD_DIM)
    return h + rms_norm(o @ w_o, g_post)


def setup_inputs(seed: int = 0) -> dict:
    key = jax.random.key(seed)
    ks = iter(jax.random.split(key, 48))

    def nrm(shape, scale):
        return jax.random.normal(next(ks), shape, jnp.float32) * scale

    def gain(shape):
        return 1.0 + nrm(shape, 0.02)

    L = DEPTH
    d = D_MODEL
    return {
        'x_prompt': nrm((BATCH, SEQ, d), 1.0),
        'x_sample': nrm((DEC_BATCH, DEC_SEQ, d), 1.0),
        'mem_prompt': nrm((BATCH, MEM_TOKENS, d), 1.0),
        'cache_mem_k': nrm((L, DEC_BATCH, MEM_TOKENS, X_HEADS, X_HEAD_DIM), 1.0),
        'cache_mem_v': nrm((L, DEC_BATCH, MEM_TOKENS, X_HEADS, X_HEAD_DIM), 1.0),
        'state_conv': nrm((L, DEC_BATCH, CONV_K - 1, CONV_WIDTH), 0.5),
        'ffn1_g_pre': gain((L, d)),
        'ffn1_w_gate': nrm((L, d, FFN_DIM), d ** -0.5),
        'ffn1_w_up': nrm((L, d, FFN_DIM), d ** -0.5),
        'ffn1_w_down': nrm((L, FFN_DIM, d), FFN_DIM ** -0.5),
        'ffn1_g_post': gain((L, d)),
        'mix_g_pre': gain((L, d)),
        'mix_w_in': nrm((L, d, IN_COLS), d ** -0.5),
        'sgu_ln_g': gain((L, SGU_WIDTH)),
        'sgu_ln_b': nrm((L, SGU_WIDTH), 0.02),
        'sgu_w_s': nrm((L, SGU_HEADS, CHUNK, CHUNK), CHUNK ** -0.5),
        'sgu_b_s': 1.0 + nrm((L, SGU_HEADS, CHUNK), 0.02),
        'conv_w': nrm((L, CONV_K, CONV_WIDTH), CONV_K ** -0.5),
        'conv_b': nrm((L, CONV_WIDTH), 0.02),
        'conv_ln_g': gain((L, CONV_WIDTH)),
        'conv_ln_b': nrm((L, CONV_WIDTH), 0.02),
        'mix_w_out': nrm((L, MIX_WIDTH, d), MIX_WIDTH ** -0.5),
        'mix_g_post': gain((L, d)),
        'mem_g': gain((L, d)),
        'xattn_g_pre': gain((L, d)),
        'xattn_w_q': nrm((L, d, X_HEADS * X_HEAD_DIM), d ** -0.5),
        'xattn_w_k': nrm((L, d, X_HEADS * X_HEAD_DIM), d ** -0.5),
        'xattn_w_v': nrm((L, d, X_HEADS * X_HEAD_DIM), d ** -0.5),
        'xattn_w_o': nrm((L, X_HEADS * X_HEAD_DIM, d), (X_HEADS * X_HEAD_DIM) ** -0.5),
        'xattn_g_post': gain((L, d)),
        'ffn2_g_pre': gain((L, d)),
        'ffn2_w_gate': nrm((L, d, FFN_DIM), d ** -0.5),
        'ffn2_w_up': nrm((L, d, FFN_DIM), d ** -0.5),
        'ffn2_w_down': nrm((L, FFN_DIM, d), FFN_DIM ** -0.5),
        'ffn2_g_post': gain((L, d)),
    }


def reference(x_prompt, x_sample, mem_prompt, cache_mem_k, cache_mem_v, state_conv,
              ffn1_g_pre, ffn1_w_gate, ffn1_w_up, ffn1_w_down, ffn1_g_post,
              mix_g_pre, mix_w_in, sgu_ln_g, sgu_ln_b, sgu_w_s, sgu_b_s,
              conv_w, conv_b, conv_ln_g, conv_ln_b, mix_w_out, mix_g_post,
              mem_g, xattn_g_pre, xattn_w_q, xattn_w_k, xattn_w_v, xattn_w_o, xattn_g_post,
              ffn2_g_pre, ffn2_w_gate, ffn2_w_up, ffn2_w_down, ffn2_g_post):

    def run_layer(x, conv_left, mem_k, mem_v, l):
        h = macaron_ffn(x, ffn1_g_pre[l], ffn1_w_gate[l], ffn1_w_up[l], ffn1_w_down[l], ffn1_g_post[l])
        h, conv_state, chunk_v = token_mixing(
            h, conv_left, mix_g_pre[l], mix_w_in[l], sgu_ln_g[l], sgu_ln_b[l], sgu_w_s[l], sgu_b_s[l],
            conv_w[l], conv_b[l], conv_ln_g[l], conv_ln_b[l], mix_w_out[l], mix_g_post[l])
        h = cross_attend(h, mem_k, mem_v, xattn_g_pre[l], xattn_w_q[l], xattn_w_o[l], xattn_g_post[l])
        h = macaron_ffn(h, ffn2_g_pre[l], ffn2_w_gate[l], ffn2_w_up[l], ffn2_w_down[l], ffn2_g_post[l])
        return h, conv_state, chunk_v

    hp, hs = x_prompt, x_sample
    mk_p, mv_p, cs_p, cs_s, cv_p, cv_s = [], [], [], [], [], []
    for l in range(DEPTH):
        k_p, v_p = memory_kv(mem_prompt, mem_g[l], xattn_w_k[l], xattn_w_v[l])
        left_p = jnp.zeros((hp.shape[0], CONV_K - 1, CONV_WIDTH), hp.dtype)
        hp, conv_p, chunk_p = run_layer(hp, left_p, k_p, v_p, l)
        hs, conv_s, chunk_s = run_layer(hs, state_conv[l], cache_mem_k[l], cache_mem_v[l], l)
        mk_p.append(k_p)
        mv_p.append(v_p)
        cs_p.append(conv_p)
        cs_s.append(conv_s)
        cv_p.append(chunk_p)
        cv_s.append(chunk_s)

    return (hp, hs, jnp.stack(mk_p), jnp.stack(mv_p), jnp.stack(cs_p), jnp.stack(cs_s),
            jnp.stack(cv_p), jnp.stack(cv_s))
```

```python
import functools

import jax
import jax.numpy as jnp
from jax import lax
from jax.experimental import pallas as pl
from jax.experimental.pallas import tpu as pltpu

D_MODEL = 1024
SGU_WIDTH = 512
CONV_WIDTH = 512
SGU_HEADS = 4
SGU_HEAD_DIM = 128
CHUNK = 128
CONV_K = 31
MEM_TOKENS = 256
X_HEADS = 4
X_HEAD_DIM = 256
FFN_DIM = 4096
EPS = 1e-6

F32 = jnp.float32
BF16 = jnp.bfloat16

FFN_ROWS = 512
FFN_COLS = 512
MIX_ROWS = 512
SAMPLE_BATCH_TILE = 8
CARRY_ROWS = 32
VMEM_LIMIT = 56 * 1024 * 1024


def _rms(x, g):
    return x * lax.rsqrt(jnp.mean(x * x, axis=-1, keepdims=True) + EPS) * g


def _ln(x, g, b):
    mu = jnp.mean(x, axis=-1, keepdims=True)
    xc = x - mu
    return xc * lax.rsqrt(jnp.mean(xc * xc, axis=-1, keepdims=True) + EPS) * g + b


def _silu(x):
    return x * jax.nn.sigmoid(x)


def _dot(a, b):
    return jnp.dot(a, b, preferred_element_type=F32)


def _dot_nt(a, b):
    return lax.dot_general(a, b, (((1,), (1,)), ((), ())), preferred_element_type=F32)


def _const_spec(shape):
    nd = len(shape)
    return pl.BlockSpec(shape, lambda *_: (0,) * nd, pipeline_mode=pl.Buffered(1))


def _ffn_kernel(x_ref, gpre_ref, wg_ref, wu_ref, wd_ref, gpost_ref, o_ref, acc_ref):
    x = x_ref[...]
    n = _rms(x, gpre_ref[...]).astype(BF16)
    for c in range(FFN_DIM // FFN_COLS):
        cols = slice(c * FFN_COLS, (c + 1) * FFN_COLS)
        g = _dot(n, wg_ref[:, cols])
        u = _dot(n, wu_ref[:, cols])
        h = (_silu(g) * u).astype(BF16)
        d = _dot(h, wd_ref[cols, :])
        if c == 0:
            acc_ref[...] = d
        else:
            acc_ref[...] += d
    o_ref[...] = x + 0.5 * _rms(acc_ref[...], gpost_ref[...])


def _ffn(x, g_pre, w_gate, w_up, w_down, g_post):
    n_rows = x.shape[0]
    return pl.pallas_call(
        _ffn_kernel,
        out_shape=jax.ShapeDtypeStruct((n_rows, D_MODEL), F32),
        grid=(n_rows // FFN_ROWS,),
        in_specs=[
            pl.BlockSpec((FFN_ROWS, D_MODEL), lambda i: (i, 0)),
            _const_spec((1, D_MODEL)),
            _const_spec((D_MODEL, FFN_DIM)),
            _const_spec((D_MODEL, FFN_DIM)),
            _const_spec((FFN_DIM, D_MODEL)),
            _const_spec((1, D_MODEL)),
        ],
        out_specs=pl.BlockSpec((FFN_ROWS, D_MODEL), lambda i: (i, 0)),
        scratch_shapes=[pltpu.VMEM((FFN_ROWS, D_MODEL), F32)],
        compiler_params=pltpu.CompilerParams(
            dimension_semantics=("arbitrary",), vmem_limit_bytes=VMEM_LIMIT),
        name="ffn",
    )(x, g_pre, w_gate, w_up, w_down, g_post)


def _mem_kv_kernel(m_ref, g_ref, wk_ref, wv_ref, k_ref, v_ref):
    mn = _rms(m_ref[...], g_ref[...]).astype(BF16)
    k_ref[...] = _dot(mn, wk_ref[...])
    v_ref[...] = _dot(mn, wv_ref[...])


def _mem_kv(mem, g, w_k, w_v):
    n_rows = mem.shape[0]
    rows = MEM_TOKENS
    out = jax.ShapeDtypeStruct((n_rows, D_MODEL), F32)
    return pl.pallas_call(
        _mem_kv_kernel,
        out_shape=(out, out),
        grid=(n_rows // rows,),
        in_specs=[
            pl.BlockSpec((rows, D_MODEL), lambda i: (i, 0)),
            _const_spec((1, D_MODEL)),
            _const_spec((D_MODEL, D_MODEL)),
            _const_spec((D_MODEL, D_MODEL)),
        ],
        out_specs=(pl.BlockSpec((rows, D_MODEL), lambda i: (i, 0)),
                   pl.BlockSpec((rows, D_MODEL), lambda i: (i, 0))),
        compiler_params=pltpu.CompilerParams(
            dimension_semantics=("arbitrary",), vmem_limit_bytes=VMEM_LIMIT),
        name="mem_kv",
    )(mem, g, w_k, w_v)


def _attend(q, k, v):
    outs = []
    for hd in range(X_HEADS):
        cols = slice(hd * X_HEAD_DIM, (hd + 1) * X_HEAD_DIM)
        s = _dot_nt(q[:, cols].astype(BF16), k[:, cols]) * (X_HEAD_DIM ** -0.5)
        e = jnp.exp(s - jnp.max(s, axis=-1, keepdims=True))
        p = e / jnp.sum(e, axis=-1, keepdims=True)
        outs.append(_dot(p.astype(BF16), v[:, cols]).astype(BF16))
    return jnp.concatenate(outs, axis=-1)


def _mix_prompt_kernel(h_ref, k_ref, v_ref, gpre_ref, win_ref, slng_ref, slnb_ref, ws_ref, bs_ref,
                       cw_ref, cb_ref, clng_ref, clnb_ref, wout_ref, gpost_ref,
                       xgpre_ref, wq_ref, wo_ref, xgpost_ref,
                       o_ref, cstate_ref, chunkv_ref, cbuf_ref):
    t = pl.program_id(1)
    n_t = pl.num_programs(1)
    rows = MIX_ROWS

    @pl.when(t == 0)
    def _():
        cbuf_ref[0:CARRY_ROWS, :] = jnp.zeros((CARRY_ROWS, CONV_WIDTH), F32)

    h = h_ref[...]
    n = _rms(h, gpre_ref[...]).astype(BF16)
    z = _dot(n, win_ref[...])
    u = z[:, 0:SGU_WIDTH]
    v = _ln(z[:, SGU_WIDTH:2 * SGU_WIDTH], slng_ref[...], slnb_ref[...])
    a = z[:, 2 * SGU_WIDTH:2 * SGU_WIDTH + CONV_WIDTH]
    gate = z[:, 2 * SGU_WIDTH + CONV_WIDTH:]

    @pl.when(t == n_t - 1)
    def _():
        chunkv_ref[...] = v[rows - CHUNK:, :]

    row_id = lax.broadcasted_iota(jnp.int32, (CHUNK, CHUNK), 0)
    col_id = lax.broadcasted_iota(jnp.int32, (CHUNK, CHUNK), 1)
    vb = v.astype(BF16)
    s_heads = []
    for hd in range(SGU_HEADS):
        ws = jnp.where(row_id >= col_id, ws_ref[hd], 0.0).astype(BF16)
        cols = slice(hd * SGU_HEAD_DIM, (hd + 1) * SGU_HEAD_DIM)
        s_chunks = [_dot(ws, vb[c * CHUNK:(c + 1) * CHUNK, cols]) for c in range(rows // CHUNK)]
        s_heads.append(jnp.concatenate(s_chunks, axis=0))
    bias = jnp.concatenate([bs_ref[...]] * (rows // CHUNK), axis=0)
    out_a = u * (jnp.concatenate(s_heads, axis=-1) + bias)

    cbuf_ref[CARRY_ROWS:CARRY_ROWS + rows, :] = a * jax.nn.sigmoid(gate)
    base = CARRY_ROWS - (CONV_K - 1)
    acc = cbuf_ref[base:base + rows, :] * cw_ref[0:1, :]
    for kk in range(1, CONV_K):
        acc = acc + cbuf_ref[base + kk:base + kk + rows, :] * cw_ref[kk:kk + 1, :]
    conv = acc + cb_ref[...]
    out_b = _silu(_ln(conv, clng_ref[...], clnb_ref[...]))

    @pl.when(t == n_t - 1)
    def _():
        cstate_ref[...] = cbuf_ref[rows + base:rows + CARRY_ROWS, :]

    cbuf_ref[0:CARRY_ROWS, :] = cbuf_ref[rows:rows + CARRY_ROWS, :]

    mixed = jnp.concatenate([out_a.astype(BF16), out_b.astype(BF16)], axis=-1)
    h2 = h + _rms(_dot(mixed, wout_ref[...]), gpost_ref[...])

    q = _dot(_rms(h2, xgpre_ref[...]).astype(BF16), wq_ref[...])
    att = _attend(q, k_ref[...].astype(BF16), v_ref[...].astype(BF16))
    o_ref[...] = h2 + _rms(_dot(att, wo_ref[...]), xgpost_ref[...])


def _mix_prompt(h, k, v, p):
    bsz, seq, _ = h.shape
    tok = pl.BlockSpec((None, MIX_ROWS, D_MODEL), lambda b, t: (b, t, 0))
    mem = pl.BlockSpec((None, MEM_TOKENS, D_MODEL), lambda b, t: (b, 0, 0))
    return pl.pallas_call(
        _mix_prompt_kernel,
        out_shape=(jax.ShapeDtypeStruct((bsz, seq, D_MODEL), F32),
                   jax.ShapeDtypeStruct((bsz, CONV_K - 1, CONV_WIDTH), F32),
                   jax.ShapeDtypeStruct((bsz, CHUNK, SGU_WIDTH), F32)),
        grid=(bsz, seq // MIX_ROWS),
        in_specs=[
            tok, mem, mem,
            _const_spec((1, D_MODEL)),
            _const_spec((D_MODEL, 2 * SGU_WIDTH + 2 * CONV_WIDTH)),
            _const_spec((1, SGU_WIDTH)), _const_spec((1, SGU_WIDTH)),
            _const_spec((SGU_HEADS, CHUNK, CHUNK)),
            _const_spec((CHUNK, SGU_WIDTH)),
            _const_spec((CONV_K, CONV_WIDTH)), _const_spec((1, CONV_WIDTH)),
            _const_spec((1, CONV_WIDTH)), _const_spec((1, CONV_WIDTH)),
            _const_spec((D_MODEL, D_MODEL)), _const_spec((1, D_MODEL)),
            _const_spec((1, D_MODEL)),
            _const_spec((D_MODEL, D_MODEL)), _const_spec((D_MODEL, D_MODEL)),
            _const_spec((1, D_MODEL)),
        ],
        out_specs=(tok,
                   pl.BlockSpec((None, CONV_K - 1, CONV_WIDTH), lambda b, t: (b, 0, 0)),
                   pl.BlockSpec((None, CHUNK, SGU_WIDTH), lambda b, t: (b, 0, 0))),
        scratch_shapes=[pltpu.VMEM((MIX_ROWS + CARRY_ROWS, CONV_WIDTH), F32)],
        compiler_params=pltpu.CompilerParams(
            dimension_semantics=("arbitrary", "arbitrary"), vmem_limit_bytes=VMEM_LIMIT),
        name="mix_prompt",
    )(h, k, v, p["g_pre"], p["w_in"], p["sln_g"], p["sln_b"], p["w_s"], p["b_s_full"],
      p["conv_w"], p["conv_b"], p["cln_g"], p["cln_b"], p["w_out"], p["g_post"],
      p["xg_pre"], p["w_q"], p["w_o"], p["xg_post"])


def _mix_sample_kernel(dec_seq, h_ref, st_ref, k_ref, v_ref, gpre_ref, win_ref, slng_ref, slnb_ref,
                       wsx_ref, bsx_ref, cwx_ref, cb_ref, clng_ref, clnb_ref, wout_ref, gpost_ref,
                       xgpre_ref, wq_ref, wo_ref, xgpost_ref,
                       o_ref, cstate_ref, chunkv_ref):
    bt = SAMPLE_BATCH_TILE
    hist = CONV_K - 1
    h = h_ref[...]
    n = _rms(h, gpre_ref[...]).astype(BF16)
    z = _dot(n, win_ref[...])
    u = z[:, 0:SGU_WIDTH]
    v = _ln(z[:, SGU_WIDTH:2 * SGU_WIDTH], slng_ref[...], slnb_ref[...])
    a = z[:, 2 * SGU_WIDTH:2 * SGU_WIDTH + CONV_WIDTH]
    gate = z[:, 2 * SGU_WIDTH + CONV_WIDTH:]
    chunkv_ref[...] = v

    v3 = v.reshape(bt, dec_seq, SGU_WIDTH)
    s = jnp.broadcast_to(bsx_ref[...][None], (bt, dec_seq, SGU_WIDTH))
    for j in range(dec_seq):
        s = s + wsx_ref[j][None] * v3[:, j:j + 1, :]
    out_a = u * s.reshape(bt * dec_seq, SGU_WIDTH)

    glu = a * jax.nn.sigmoid(gate)
    glu3 = glu.reshape(bt, dec_seq, CONV_WIDTH)
    acc = jnp.broadcast_to(cb_ref[...][None], (bt, dec_seq, CONV_WIDTH))
    for r in range(hist):
        acc = acc + cwx_ref[r][None] * st_ref[:, r:r + 1, :]
    for r in range(dec_seq):
        acc = acc + cwx_ref[hist + r][None] * glu3[:, r:r + 1, :]
    conv = acc.reshape(bt * dec_seq, CONV_WIDTH)
    out_b = _silu(_ln(conv, clng_ref[...], clnb_ref[...]))
    cstate_ref[:, 0:hist - dec_seq, :] = st_ref[:, dec_seq:hist, :]
    cstate_ref[:, hist - dec_seq:hist, :] = glu3

    mixed = jnp.concatenate([out_a.astype(BF16), out_b.astype(BF16)], axis=-1)
    h2 = h + _rms(_dot(mixed, wout_ref[...]), gpost_ref[...])

    q = _dot(_rms(h2, xgpre_ref[...]).astype(BF16), wq_ref[...])
    atts = []
    for b in range(bt):
        atts.append(_attend(q[b * dec_seq:(b + 1) * dec_seq, :],
                            k_ref[b].astype(BF16), v_ref[b].astype(BF16)))
    att = jnp.concatenate(atts, axis=0)
    o_ref[...] = h2 + _rms(_dot(att, wo_ref[...]), xgpost_ref[...])


def _mix_sample(h, state, k, v, p, dec_seq):
    n_rows = h.shape[0]
    bsz = n_rows // dec_seq
    bt = SAMPLE_BATCH_TILE
    hist = CONV_K - 1
    tok = pl.BlockSpec((bt * dec_seq, D_MODEL), lambda i: (i, 0))
    mem = pl.BlockSpec((bt, MEM_TOKENS, D_MODEL), lambda i: (i, 0, 0))
    st = pl.BlockSpec((bt, hist, CONV_WIDTH), lambda i: (i, 0, 0))
    return pl.pallas_call(
        functools.partial(_mix_sample_kernel, dec_seq),
        out_shape=(jax.ShapeDtypeStruct((n_rows, D_MODEL), F32),
                   jax.ShapeDtypeStruct((bsz, hist, CONV_WIDTH), F32),
                   jax.ShapeDtypeStruct((n_rows, SGU_WIDTH), F32)),
        grid=(bsz // bt,),
        in_specs=[
            tok, st, mem, mem,
            _const_spec((1, D_MODEL)),
            _const_spec((D_MODEL, 2 * SGU_WIDTH + 2 * CONV_WIDTH)),
            _const_spec((1, SGU_WIDTH)), _const_spec((1, SGU_WIDTH)),
            _const_spec((dec_seq, dec_seq, SGU_WIDTH)),
            _const_spec((dec_seq, SGU_WIDTH)),
            _const_spec((hist + dec_seq, dec_seq, CONV_WIDTH)), _const_spec((1, CONV_WIDTH)),
            _const_spec((1, CONV_WIDTH)), _const_spec((1, CONV_WIDTH)),
            _const_spec((D_MODEL, D_MODEL)), _const_spec((1, D_MODEL)),
            _const_spec((1, D_MODEL)),
            _const_spec((D_MODEL, D_MODEL)), _const_spec((D_MODEL, D_MODEL)),
            _const_spec((1, D_MODEL)),
        ],
        out_specs=(tok, st, pl.BlockSpec((bt * dec_seq, SGU_WIDTH), lambda i: (i, 0))),
        compiler_params=pltpu.CompilerParams(
            dimension_semantics=("arbitrary",), vmem_limit_bytes=VMEM_LIMIT),
        name="mix_sample",
    )(h, state, k, v, p["g_pre"], p["w_in"], p["sln_g"], p["sln_b"], p["w_s_rows"], p["b_s_rows"],
      p["conv_w_rows"], p["conv_b"], p["cln_g"], p["cln_b"], p["w_out"], p["g_post"],
      p["xg_pre"], p["w_q"], p["w_o"], p["xg_post"])


def _row(x):
    return x.reshape(1, -1)


def kernel(x_prompt, x_sample, mem_prompt, cache_mem_k, cache_mem_v, state_conv, ffn1_g_pre, ffn1_w_gate, ffn1_w_up, ffn1_w_down, ffn1_g_post, mix_g_pre, mix_w_in, sgu_ln_g, sgu_ln_b, sgu_w_s, sgu_b_s, conv_w, conv_b, conv_ln_g, conv_ln_b, mix_w_out, mix_g_post, mem_g, xattn_g_pre, xattn_w_q, xattn_w_k, xattn_w_v, xattn_w_o, xattn_g_post, ffn2_g_pre, ffn2_w_gate, ffn2_w_up, ffn2_w_down, ffn2_g_post):
    depth = ffn1_g_pre.shape[0]
    bsz, seq, _ = x_prompt.shape
    dbsz, dec_seq, _ = x_sample.shape
    hist = CONV_K - 1
    assert seq % MIX_ROWS == 0 and dec_seq <= CHUNK and dec_seq <= hist

    hp = x_prompt.reshape(bsz * seq, D_MODEL)
    hs = x_sample.reshape(dbsz * dec_seq, D_MODEL)
    mem = mem_prompt.reshape(bsz * MEM_TOKENS, D_MODEL)
    outs = [[] for _ in range(6)]
    for l in range(depth):
        ffn1 = (_row(ffn1_g_pre[l]), ffn1_w_gate[l].astype(BF16), ffn1_w_up[l].astype(BF16),
                ffn1_w_down[l].astype(BF16), _row(ffn1_g_post[l]))
        ffn2 = (_row(ffn2_g_pre[l]), ffn2_w_gate[l].astype(BF16), ffn2_w_up[l].astype(BF16),
                ffn2_w_down[l].astype(BF16), _row(ffn2_g_post[l]))
        w_s, b_s, cw = sgu_w_s[l], sgu_b_s[l], conv_w[l]
        tril = jnp.tril(w_s[:, :dec_seq, :dec_seq])
        w_s_rows = jnp.repeat(jnp.transpose(tril, (2, 1, 0)), SGU_HEAD_DIM, axis=2)
        b_s_rows = jnp.repeat(b_s[:, :dec_seq].T, SGU_HEAD_DIM, axis=1)
        tap = jnp.arange(hist + dec_seq)[:, None] - jnp.arange(dec_seq)[None, :]
        conv_w_rows = jnp.where(((tap >= 0) & (tap < CONV_K))[:, :, None],
                                cw[jnp.clip(tap, 0, CONV_K - 1)], 0.0)
        p = dict(
            g_pre=_row(mix_g_pre[l]), w_in=mix_w_in[l].astype(BF16),
            sln_g=_row(sgu_ln_g[l]), sln_b=_row(sgu_ln_b[l]),
            w_s=w_s, b_s_full=jnp.repeat(b_s.T, SGU_HEAD_DIM, axis=1),
            w_s_rows=w_s_rows, b_s_rows=b_s_rows,
            conv_w=cw, conv_w_rows=conv_w_rows, conv_b=_row(conv_b[l]),
            cln_g=_row(conv_ln_g[l]), cln_b=_row(conv_ln_b[l]),
            w_out=mix_w_out[l].astype(BF16), g_post=_row(mix_g_post[l]),
            xg_pre=_row(xattn_g_pre[l]), w_q=xattn_w_q[l].astype(BF16),
            w_o=xattn_w_o[l].astype(BF16), xg_post=_row(xattn_g_post[l]),
        )

        k_p, v_p = _mem_kv(mem, _row(mem_g[l]), xattn_w_k[l].astype(BF16), xattn_w_v[l].astype(BF16))

        hp = _ffn(hp, *ffn1)
        hp, cs_p, cv_p = _mix_prompt(hp.reshape(bsz, seq, D_MODEL),
                                     k_p.reshape(bsz, MEM_TOKENS, D_MODEL),
                                     v_p.reshape(bsz, MEM_TOKENS, D_MODEL), p)
        hp = _ffn(hp.reshape(bsz * seq, D_MODEL), *ffn2)

        hs = _ffn(hs, *ffn1)
        hs, cs_s, cv_s = _mix_sample(hs, state_conv[l],
                                     cache_mem_k[l].reshape(dbsz, MEM_TOKENS, D_MODEL),
                                     cache_mem_v[l].reshape(dbsz, MEM_TOKENS, D_MODEL), p, dec_seq)
        hs = _ffn(hs, *ffn2)

        outs[0].append(k_p.reshape(bsz, MEM_TOKENS, X_HEADS, X_HEAD_DIM))
        outs[1].append(v_p.reshape(bsz, MEM_TOKENS, X_HEADS, X_HEAD_DIM))
        outs[2].append(cs_p)
        outs[3].append(cs_s)
        outs[4].append(cv_p)
        outs[5].append(cv_s.reshape(dbsz, dec_seq, SGU_WIDTH))

    return (hp.reshape(bsz, seq, D_MODEL), hs.reshape(dbsz, dec_seq, D_MODEL),
            jnp.stack(outs[0]), jnp.stack(outs[1]), jnp.stack(outs[2]), jnp.stack(outs[3]),
            jnp.stack(outs[4]), jnp.stack(outs[5]))
```

```python
import functools

import jax
import jax.numpy as jnp
from jax import lax
from jax.experimental import pallas as pl
from jax.experimental.pallas import tpu as pltpu

D_MODEL = 1024
SGU_WIDTH = 512
CONV_WIDTH = 512
SGU_HEADS = 4
SGU_HEAD_DIM = 128
CHUNK = 128
CONV_K = 31
MEM_TOKENS = 256
X_HEADS = 4
X_HEAD_DIM = 256
FFN_DIM = 4096
EPS = 1e-6
LANES = 128

F32 = jnp.float32
BF16 = jnp.bfloat16

FFN_ROWS = 512
FFN_COLS = 512
MIX_ROWS = 512
SAMPLE_BATCH_TILE = 8
CARRY_ROWS = 32
VMEM_LIMIT = 56 * 1024 * 1024


def _rms(x, g):
    return x * lax.rsqrt(jnp.mean(x * x, axis=-1, keepdims=True) + EPS) * g


def _ln(x, g, b):
    mu = jnp.mean(x, axis=-1, keepdims=True)
    xc = x - mu
    return xc * lax.rsqrt(jnp.mean(xc * xc, axis=-1, keepdims=True) + EPS) * g + b


def _silu(x):
    return x * jax.nn.sigmoid(x)


def _dot(a, b):
    return jnp.dot(a, b, preferred_element_type=F32)


def _dot_nt(a, b):
    return lax.dot_general(a, b, (((1,), (1,)), ((), ())), preferred_element_type=F32)


def _const_spec(shape):
    nd = len(shape)
    return pl.BlockSpec(shape, lambda *_: (0,) * nd, pipeline_mode=pl.Buffered(1))


def _ffn_kernel(x_ref, gpre_ref, wg_ref, wu_ref, wd_ref, gpost_ref, o_ref, acc_ref):
    x = x_ref[...]
    n = _rms(x, gpre_ref[...]).astype(BF16)
    for c in range(FFN_DIM // FFN_COLS):
        cols = slice(c * FFN_COLS, (c + 1) * FFN_COLS)
        g = _dot(n, wg_ref[:, cols])
        u = _dot(n, wu_ref[:, cols])
        h = (_silu(g) * u).astype(BF16)
        d = _dot(h, wd_ref[cols, :])
        if c == 0:
            acc_ref[...] = d
        else:
            acc_ref[...] += d
    o_ref[...] = x + 0.5 * _rms(acc_ref[...], gpost_ref[...])


def _ffn(x, g_pre, w_gate, w_up, w_down, g_post):
    n_rows = x.shape[0]
    return pl.pallas_call(
        _ffn_kernel,
        out_shape=jax.ShapeDtypeStruct((n_rows, D_MODEL), F32),
        grid=(n_rows // FFN_ROWS,),
        in_specs=[
            pl.BlockSpec((FFN_ROWS, D_MODEL), lambda i: (i, 0)),
            _const_spec((1, D_MODEL)),
            _const_spec((D_MODEL, FFN_DIM)),
            _const_spec((D_MODEL, FFN_DIM)),
            _const_spec((FFN_DIM, D_MODEL)),
            _const_spec((1, D_MODEL)),
        ],
        out_specs=pl.BlockSpec((FFN_ROWS, D_MODEL), lambda i: (i, 0)),
        scratch_shapes=[pltpu.VMEM((FFN_ROWS, D_MODEL), F32)],
        compiler_params=pltpu.CompilerParams(
            dimension_semantics=("arbitrary",), vmem_limit_bytes=VMEM_LIMIT),
        name="ffn",
    )(x, g_pre, w_gate, w_up, w_down, g_post)


def _mem_kv_kernel(m_ref, g_ref, wk_ref, wv_ref, k_ref, v_ref):
    mn = _rms(m_ref[...], g_ref[...]).astype(BF16)
    k_ref[...] = _dot(mn, wk_ref[...])
    v_ref[...] = _dot(mn, wv_ref[...])


def _mem_kv(mem, g, w_k, w_v):
    n_rows = mem.shape[0]
    rows = MEM_TOKENS
    out = jax.ShapeDtypeStruct((n_rows, D_MODEL), F32)
    return pl.pallas_call(
        _mem_kv_kernel,
        out_shape=(out, out),
        grid=(n_rows // rows,),
        in_specs=[
            pl.BlockSpec((rows, D_MODEL), lambda i: (i, 0)),
            _const_spec((1, D_MODEL)),
            _const_spec((D_MODEL, D_MODEL)),
            _const_spec((D_MODEL, D_MODEL)),
        ],
        out_specs=(pl.BlockSpec((rows, D_MODEL), lambda i: (i, 0)),
                   pl.BlockSpec((rows, D_MODEL), lambda i: (i, 0))),
        compiler_params=pltpu.CompilerParams(
            dimension_semantics=("arbitrary",), vmem_limit_bytes=VMEM_LIMIT),
        name="mem_kv",
    )(mem, g, w_k, w_v)


def _attend(q, k, v):
    outs = []
    for hd in range(X_HEADS):
        cols = slice(hd * X_HEAD_DIM, (hd + 1) * X_HEAD_DIM)
        s = _dot_nt(q[:, cols].astype(BF16), k[:, cols]) * (X_HEAD_DIM ** -0.5)
        e = jnp.exp(s - jnp.max(s, axis=-1, keepdims=True))
        p = e / jnp.sum(e, axis=-1, keepdims=True)
        outs.append(_dot(p.astype(BF16), v[:, cols]).astype(BF16))
    return jnp.concatenate(outs, axis=-1)


def _mix_prompt_kernel(h_ref, k_ref, v_ref, gpre_ref, win_ref, slng_ref, slnb_ref, ws_ref, bs_ref,
                       cw_ref, cb_ref, clng_ref, clnb_ref, wout_ref, gpost_ref,
                       xgpre_ref, wq_ref, wo_ref, xgpost_ref,
                       o_ref, cstate_ref, chunkv_ref, cbuf_ref):
    t = pl.program_id(1)
    n_t = pl.num_programs(1)
    rows = MIX_ROWS

    @pl.when(t == 0)
    def _():
        cbuf_ref[0:CARRY_ROWS, :] = jnp.zeros((CARRY_ROWS, CONV_WIDTH), F32)

    h = h_ref[...]
    n = _rms(h, gpre_ref[...]).astype(BF16)
    z = _dot(n, win_ref[...])
    u = z[:, 0:SGU_WIDTH]
    v = _ln(z[:, SGU_WIDTH:2 * SGU_WIDTH], slng_ref[...], slnb_ref[...])
    a = z[:, 2 * SGU_WIDTH:2 * SGU_WIDTH + CONV_WIDTH]
    gate = z[:, 2 * SGU_WIDTH + CONV_WIDTH:]

    @pl.when(t == n_t - 1)
    def _():
        chunkv_ref[...] = v[rows - CHUNK:, :]

    row_id = lax.broadcasted_iota(jnp.int32, (CHUNK, CHUNK), 0)
    col_id = lax.broadcasted_iota(jnp.int32, (CHUNK, CHUNK), 1)
    vb = v.astype(BF16)
    s_heads = []
    for hd in range(SGU_HEADS):
        ws = jnp.where(row_id >= col_id, ws_ref[hd], 0.0).astype(BF16)
        cols = slice(hd * SGU_HEAD_DIM, (hd + 1) * SGU_HEAD_DIM)
        s_chunks = [_dot(ws, vb[c * CHUNK:(c + 1) * CHUNK, cols]) for c in range(rows // CHUNK)]
        s_heads.append(jnp.concatenate(s_chunks, axis=0))
    bias = jnp.concatenate([bs_ref[...]] * (rows // CHUNK), axis=0)
    out_a = u * (jnp.concatenate(s_heads, axis=-1) + bias)

    cbuf_ref[CARRY_ROWS:CARRY_ROWS + rows, :] = a * jax.nn.sigmoid(gate)
    base = CARRY_ROWS - (CONV_K - 1)
    acc = cbuf_ref[base:base + rows, :] * cw_ref[0:1, :]
    for kk in range(1, CONV_K):
        acc = acc + cbuf_ref[base + kk:base + kk + rows, :] * cw_ref[kk:kk + 1, :]
    conv = acc + cb_ref[...]
    out_b = _silu(_ln(conv, clng_ref[...], clnb_ref[...]))

    @pl.when(t == n_t - 1)
    def _():
        cstate_ref[...] = cbuf_ref[rows + base:rows + CARRY_ROWS, :]

    cbuf_ref[0:CARRY_ROWS, :] = cbuf_ref[rows:rows + CARRY_ROWS, :]

    mixed = jnp.concatenate([out_a.astype(BF16), out_b.astype(BF16)], axis=-1)
    h2 = h + _rms(_dot(mixed, wout_ref[...]), gpost_ref[...])

    q = _dot(_rms(h2, xgpre_ref[...]).astype(BF16), wq_ref[...])
    att = _attend(q, k_ref[...].astype(BF16), v_ref[...].astype(BF16))
    o_ref[...] = h2 + _rms(_dot(att, wo_ref[...]), xgpost_ref[...])


def _mix_prompt(h, k, v, p):
    bsz, seq, _ = h.shape
    tok = pl.BlockSpec((None, MIX_ROWS, D_MODEL), lambda b, t: (b, t, 0))
    mem = pl.BlockSpec((None, MEM_TOKENS, D_MODEL), lambda b, t: (b, 0, 0))
    return pl.pallas_call(
        _mix_prompt_kernel,
        out_shape=(jax.ShapeDtypeStruct((bsz, seq, D_MODEL), F32),
                   jax.ShapeDtypeStruct((bsz, CONV_K - 1, CONV_WIDTH), F32),
                   jax.ShapeDtypeStruct((bsz, CHUNK, SGU_WIDTH), F32)),
        grid=(bsz, seq // MIX_ROWS),
        in_specs=[
            tok, mem, mem,
            _const_spec((1, D_MODEL)),
            _const_spec((D_MODEL, 2 * SGU_WIDTH + 2 * CONV_WIDTH)),
            _const_spec((1, SGU_WIDTH)), _const_spec((1, SGU_WIDTH)),
            _const_spec((SGU_HEADS, CHUNK, CHUNK)),
            _const_spec((CHUNK, SGU_WIDTH)),
            _const_spec((CONV_K, CONV_WIDTH)), _const_spec((1, CONV_WIDTH)),
            _const_spec((1, CONV_WIDTH)), _const_spec((1, CONV_WIDTH)),
            _const_spec((D_MODEL, D_MODEL)), _const_spec((1, D_MODEL)),
            _const_spec((1, D_MODEL)),
            _const_spec((D_MODEL, D_MODEL)), _const_spec((D_MODEL, D_MODEL)),
            _const_spec((1, D_MODEL)),
        ],
        out_specs=(tok,
                   pl.BlockSpec((None, CONV_K - 1, CONV_WIDTH), lambda b, t: (b, 0, 0)),
                   pl.BlockSpec((None, CHUNK, SGU_WIDTH), lambda b, t: (b, 0, 0))),
        scratch_shapes=[pltpu.VMEM((MIX_ROWS + CARRY_ROWS, CONV_WIDTH), F32)],
        compiler_params=pltpu.CompilerParams(
            dimension_semantics=("arbitrary", "arbitrary"), vmem_limit_bytes=VMEM_LIMIT),
        name="mix_prompt",
    )(h, k, v, p["g_pre"], p["w_in"], p["sln_g"], p["sln_b"], p["w_s"], p["b_s_full"],
      p["conv_w"], p["conv_b"], p["cln_g"], p["cln_b"], p["w_out"], p["g_post"],
      p["xg_pre"], p["w_q"], p["w_o"], p["xg_post"])


def _cached_head(c_ref, b, hd):
    halves = X_HEAD_DIM // LANES
    stride = X_HEADS * halves
    parts = [c_ref[b, pl.ds(j * X_HEADS + hd, MEM_TOKENS, stride=stride), :] for j in range(halves)]
    return jnp.concatenate(parts, axis=-1).astype(BF16)


def _mix_sample_kernel(dec_seq, h_ref, st_ref, k_ref, v_ref, gpre_ref, win_ref, slng_ref, slnb_ref,
                       wsx_ref, bsx_ref, cwx_ref, cb_ref, clng_ref, clnb_ref, wout_ref, gpost_ref,
                       xgpre_ref, wq_ref, wo_ref, xgpost_ref,
                       o_ref, cstate_ref, chunkv_ref, glu_ref):
    bt = SAMPLE_BATCH_TILE
    hist = CONV_K - 1
    h = h_ref[...]
    n = _rms(h, gpre_ref[...]).astype(BF16)
    z = _dot(n, win_ref[...])
    u = z[:, 0:SGU_WIDTH]
    v = _ln(z[:, SGU_WIDTH:2 * SGU_WIDTH], slng_ref[...], slnb_ref[...])
    a = z[:, 2 * SGU_WIDTH:2 * SGU_WIDTH + CONV_WIDTH]
    gate = z[:, 2 * SGU_WIDTH + CONV_WIDTH:]
    chunkv_ref[...] = v

    v3 = v.reshape(bt, dec_seq, SGU_WIDTH)
    s = jnp.broadcast_to(bsx_ref[...][None], (bt, dec_seq, SGU_WIDTH))
    for j in range(dec_seq):
        s = s + wsx_ref[j][None] * v3[:, j:j + 1, :]
    out_a = u * s.reshape(bt * dec_seq, SGU_WIDTH)

    glu = a * jax.nn.sigmoid(gate)
    glu3 = glu.reshape(bt, dec_seq, CONV_WIDTH)
    new_part = jnp.broadcast_to(cb_ref[...][None], (bt, dec_seq, CONV_WIDTH))
    for r in range(dec_seq):
        new_part = new_part + cwx_ref[hist + r][None] * glu3[:, r:r + 1, :]
    old_parts = []
    for b in range(bt):
        acc = cwx_ref[0] * st_ref[0, b:b + 1, :]
        for r in range(1, hist):
            acc = acc + cwx_ref[r] * st_ref[r, b:b + 1, :]
        old_parts.append(acc)
    conv = new_part.reshape(bt * dec_seq, CONV_WIDTH) + jnp.concatenate(old_parts, axis=0)
    out_b = _silu(_ln(conv, clng_ref[...], clnb_ref[...]))

    cstate_ref[0:hist - dec_seq, :, :] = st_ref[dec_seq:hist, :, :]
    for sl in range(CONV_WIDTH // LANES):
        glu_ref[sl] = glu[:, sl * LANES:(sl + 1) * LANES]
    for tt in range(dec_seq):
        for sl in range(CONV_WIDTH // LANES):
            cstate_ref[hist - dec_seq + tt, :, sl * LANES:(sl + 1) * LANES] = (
                glu_ref[sl, pl.ds(tt, bt, stride=dec_seq), :])

    mixed = jnp.concatenate([out_a.astype(BF16), out_b.astype(BF16)], axis=-1)
    h2 = h + _rms(_dot(mixed, wout_ref[...]), gpost_ref[...])

    q = _dot(_rms(h2, xgpre_ref[...]).astype(BF16), wq_ref[...])
    atts = []
    for b in range(bt):
        qb = q[b * dec_seq:(b + 1) * dec_seq, :]
        heads = []
        for hd in range(X_HEADS):
            cols = slice(hd * X_HEAD_DIM, (hd + 1) * X_HEAD_DIM)
            sc = _dot_nt(qb[:, cols].astype(BF16), _cached_head(k_ref, b, hd)) * (X_HEAD_DIM ** -0.5)
            e = jnp.exp(sc - jnp.max(sc, axis=-1, keepdims=True))
            pr = e / jnp.sum(e, axis=-1, keepdims=True)
            heads.append(_dot(pr.astype(BF16), _cached_head(v_ref, b, hd)).astype(BF16))
        atts.append(jnp.concatenate(heads, axis=-1))
    att = jnp.concatenate(atts, axis=0)
    o_ref[...] = h2 + _rms(_dot(att, wo_ref[...]), xgpost_ref[...])


def _mix_sample(h, state, k, v, p, dec_seq):
    n_rows = h.shape[0]
    bsz = n_rows // dec_seq
    bt = SAMPLE_BATCH_TILE
    hist = CONV_K - 1
    tok = pl.BlockSpec((bt * dec_seq, D_MODEL), lambda i: (i, 0))
    mem = pl.BlockSpec((bt,) + k.shape[1:], lambda i: (i, 0, 0))
    st = pl.BlockSpec((hist, bt, CONV_WIDTH), lambda i: (0, i, 0))
    return pl.pallas_call(
        functools.partial(_mix_sample_kernel, dec_seq),
        out_shape=(jax.ShapeDtypeStruct((n_rows, D_MODEL), F32),
                   jax.ShapeDtypeStruct((hist, bsz, CONV_WIDTH), F32),
                   jax.ShapeDtypeStruct((n_rows, SGU_WIDTH), F32)),
        grid=(bsz // bt,),
        in_specs=[
            tok, st, mem, mem,
            _const_spec((1, D_MODEL)),
            _const_spec((D_MODEL, 2 * SGU_WIDTH + 2 * CONV_WIDTH)),
            _const_spec((1, SGU_WIDTH)), _const_spec((1, SGU_WIDTH)),
            _const_spec((dec_seq, dec_seq, SGU_WIDTH)),
            _const_spec((dec_seq, SGU_WIDTH)),
            _const_spec((hist + dec_seq, dec_seq, CONV_WIDTH)), _const_spec((1, CONV_WIDTH)),
            _const_spec((1, CONV_WIDTH)), _const_spec((1, CONV_WIDTH)),
            _const_spec((D_MODEL, D_MODEL)), _const_spec((1, D_MODEL)),
            _const_spec((1, D_MODEL)),
            _const_spec((D_MODEL, D_MODEL)), _const_spec((D_MODEL, D_MODEL)),
            _const_spec((1, D_MODEL)),
        ],
        out_specs=(tok, st, pl.BlockSpec((bt * dec_seq, SGU_WIDTH), lambda i: (i, 0))),
        scratch_shapes=[pltpu.VMEM((CONV_WIDTH // LANES, bt * dec_seq, LANES), F32)],
        compiler_params=pltpu.CompilerParams(
            dimension_semantics=("arbitrary",), vmem_limit_bytes=VMEM_LIMIT),
        name="mix_sample",
    )(h, state, k, v, p["g_pre"], p["w_in"], p["sln_g"], p["sln_b"], p["w_s_rows"], p["b_s_rows"],
      p["conv_w_rows"], p["conv_b"], p["cln_g"], p["cln_b"], p["w_out"], p["g_post"],
      p["xg_pre"], p["w_q"], p["w_o"], p["xg_post"])


def _head_tiled_rows(c):
    b, m, hh, dh = c.shape
    c = c.reshape(b, m, hh, dh // LANES, LANES)
    return jnp.transpose(c, (0, 1, 3, 2, 4)).reshape(b, m * hh * (dh // LANES), LANES)


def _row(x):
    return x.reshape(1, -1)


def kernel(x_prompt, x_sample, mem_prompt, cache_mem_k, cache_mem_v, state_conv, ffn1_g_pre, ffn1_w_gate, ffn1_w_up, ffn1_w_down, ffn1_g_post, mix_g_pre, mix_w_in, sgu_ln_g, sgu_ln_b, sgu_w_s, sgu_b_s, conv_w, conv_b, conv_ln_g, conv_ln_b, mix_w_out, mix_g_post, mem_g, xattn_g_pre, xattn_w_q, xattn_w_k, xattn_w_v, xattn_w_o, xattn_g_post, ffn2_g_pre, ffn2_w_gate, ffn2_w_up, ffn2_w_down, ffn2_g_post):
    depth = ffn1_g_pre.shape[0]
    bsz, seq, _ = x_prompt.shape
    dbsz, dec_seq, _ = x_sample.shape
    hist = CONV_K - 1
    assert seq % MIX_ROWS == 0 and dec_seq <= CHUNK and dec_seq <= hist

    hp = x_prompt.reshape(bsz * seq, D_MODEL)
    hs = x_sample.reshape(dbsz * dec_seq, D_MODEL)
    mem = mem_prompt.reshape(bsz * MEM_TOKENS, D_MODEL)
    outs = [[] for _ in range(6)]
    for l in range(depth):
        ffn1 = (_row(ffn1_g_pre[l]), ffn1_w_gate[l].astype(BF16), ffn1_w_up[l].astype(BF16),
                ffn1_w_down[l].astype(BF16), _row(ffn1_g_post[l]))
        ffn2 = (_row(ffn2_g_pre[l]), ffn2_w_gate[l].astype(BF16), ffn2_w_up[l].astype(BF16),
                ffn2_w_down[l].astype(BF16), _row(ffn2_g_post[l]))
        w_s, b_s, cw = sgu_w_s[l], sgu_b_s[l], conv_w[l]
        tril = jnp.tril(w_s[:, :dec_seq, :dec_seq])
        w_s_rows = jnp.repeat(jnp.transpose(tril, (2, 1, 0)), SGU_HEAD_DIM, axis=2)
        b_s_rows = jnp.repeat(b_s[:, :dec_seq].T, SGU_HEAD_DIM, axis=1)
        tap = jnp.arange(hist + dec_seq)[:, None] - jnp.arange(dec_seq)[None, :]
        conv_w_rows = jnp.where(((tap >= 0) & (tap < CONV_K))[:, :, None],
                                cw[jnp.clip(tap, 0, CONV_K - 1)], 0.0)
        p = dict(
            g_pre=_row(mix_g_pre[l]), w_in=mix_w_in[l].astype(BF16),
            sln_g=_row(sgu_ln_g[l]), sln_b=_row(sgu_ln_b[l]),
            w_s=w_s, b_s_full=jnp.repeat(b_s.T, SGU_HEAD_DIM, axis=1),
            w_s_rows=w_s_rows, b_s_rows=b_s_rows,
            conv_w=cw, conv_w_rows=conv_w_rows, conv_b=_row(conv_b[l]),
            cln_g=_row(conv_ln_g[l]), cln_b=_row(conv_ln_b[l]),
            w_out=mix_w_out[l].astype(BF16), g_post=_row(mix_g_post[l]),
            xg_pre=_row(xattn_g_pre[l]), w_q=xattn_w_q[l].astype(BF16),
            w_o=xattn_w_o[l].astype(BF16), xg_post=_row(xattn_g_post[l]),
        )

        k_p, v_p = _mem_kv(mem, _row(mem_g[l]), xattn_w_k[l].astype(BF16), xattn_w_v[l].astype(BF16))

        hp = _ffn(hp, *ffn1)
        hp, cs_p, cv_p = _mix_prompt(hp.reshape(bsz, seq, D_MODEL),
                                     k_p.reshape(bsz, MEM_TOKENS, D_MODEL),
                                     v_p.reshape(bsz, MEM_TOKENS, D_MODEL), p)
        hp = _ffn(hp.reshape(bsz * seq, D_MODEL), *ffn2)

        hs = _ffn(hs, *ffn1)
        hs, cs_s, cv_s = _mix_sample(hs, jnp.transpose(state_conv[l], (1, 0, 2)),
                                     _head_tiled_rows(cache_mem_k[l]),
                                     _head_tiled_rows(cache_mem_v[l]), p, dec_seq)
        hs = _ffn(hs, *ffn2)

        outs[0].append(k_p.reshape(bsz, MEM_TOKENS, X_HEADS, X_HEAD_DIM))
        outs[1].append(v_p.reshape(bsz, MEM_TOKENS, X_HEADS, X_HEAD_DIM))
        outs[2].append(cs_p)
        outs[3].append(jnp.transpose(cs_s, (1, 0, 2)))
        outs[4].append(cv_p)
        outs[5].append(cv_s.reshape(dbsz, dec_seq, SGU_WIDTH))

    return (hp.reshape(bsz, seq, D_MODEL), hs.reshape(dbsz, dec_seq, D_MODEL),
            jnp.stack(outs[0]), jnp.stack(outs[1]), jnp.stack(outs[2]), jnp.stack(outs[3]),
            jnp.stack(outs[4]), jnp.stack(outs[5]))
```

```python
import functools

import jax
import jax.numpy as jnp
from jax import lax
from jax.experimental import pallas as pl
from jax.experimental.pallas import tpu as pltpu

D_MODEL = 1024
SGU_WIDTH = 512
CONV_WIDTH = 512
SGU_HEADS = 4
SGU_HEAD_DIM = 128
CHUNK = 128
CONV_K = 31
MEM_TOKENS = 256
X_HEADS = 4
X_HEAD_DIM = 256
FFN_DIM = 4096
EPS = 1e-6
LANES = 128
SUBLANES = 8

F32 = jnp.float32
BF16 = jnp.bfloat16

FFN_ROWS = 512
FFN_COLS = 512
MIX_ROWS = 512
SAMPLE_BATCH_TILE = 8
CARRY_ROWS = 32
CONV_BLOCK_ROWS = 128
VMEM_LIMIT = 56 * 1024 * 1024


def _rms(x, g):
    return x * lax.rsqrt(jnp.mean(x * x, axis=-1, keepdims=True) + EPS) * g


def _ln(x, g, b):
    mu = jnp.mean(x, axis=-1, keepdims=True)
    xc = x - mu
    return xc * lax.rsqrt(jnp.mean(xc * xc, axis=-1, keepdims=True) + EPS) * g + b


def _silu(x):
    return x * jax.nn.sigmoid(x)


def _dot(a, b):
    return jnp.dot(a, b, preferred_element_type=F32)


def _dot_nt(a, b):
    return lax.dot_general(a, b, (((1,), (1,)), ((), ())), preferred_element_type=F32)


def _const_spec(shape):
    nd = len(shape)
    return pl.BlockSpec(shape, lambda *_: (0,) * nd, pipeline_mode=pl.Buffered(1))


def _ffn_kernel(x_ref, gpre_ref, wg_ref, wu_ref, wd_ref, gpost_ref, o_ref, acc_ref):
    x = x_ref[...]
    n = _rms(x, gpre_ref[...]).astype(BF16)
    for c in range(FFN_DIM // FFN_COLS):
        cols = slice(c * FFN_COLS, (c + 1) * FFN_COLS)
        g = _dot(n, wg_ref[:, cols])
        u = _dot(n, wu_ref[:, cols])
        h = (_silu(g) * u).astype(BF16)
        d = _dot(h, wd_ref[cols, :])
        if c == 0:
            acc_ref[...] = d
        else:
            acc_ref[...] += d
    o_ref[...] = x + 0.5 * _rms(acc_ref[...], gpost_ref[...])


def _ffn(x, g_pre, w_gate, w_up, w_down, g_post):
    n_rows = x.shape[0]
    return pl.pallas_call(
        _ffn_kernel,
        out_shape=jax.ShapeDtypeStruct((n_rows, D_MODEL), F32),
        grid=(n_rows // FFN_ROWS,),
        in_specs=[
            pl.BlockSpec((FFN_ROWS, D_MODEL), lambda i: (i, 0)),
            _const_spec((1, D_MODEL)),
            _const_spec((D_MODEL, FFN_DIM)),
            _const_spec((D_MODEL, FFN_DIM)),
            _const_spec((FFN_DIM, D_MODEL)),
            _const_spec((1, D_MODEL)),
        ],
        out_specs=pl.BlockSpec((FFN_ROWS, D_MODEL), lambda i: (i, 0)),
        scratch_shapes=[pltpu.VMEM((FFN_ROWS, D_MODEL), F32)],
        compiler_params=pltpu.CompilerParams(
            dimension_semantics=("arbitrary",), vmem_limit_bytes=VMEM_LIMIT),
        name="ffn",
    )(x, g_pre, w_gate, w_up, w_down, g_post)


def _mem_kv_kernel(m_ref, g_ref, wk_ref, wv_ref, k_ref, v_ref):
    mn = _rms(m_ref[...], g_ref[...]).astype(BF16)
    k_ref[...] = _dot(mn, wk_ref[...])
    v_ref[...] = _dot(mn, wv_ref[...])


def _mem_kv(mem, g, w_k, w_v):
    n_rows = mem.shape[0]
    rows = MEM_TOKENS
    out = jax.ShapeDtypeStruct((n_rows, D_MODEL), F32)
    return pl.pallas_call(
        _mem_kv_kernel,
        out_shape=(out, out),
        grid=(n_rows // rows,),
        in_specs=[
            pl.BlockSpec((rows, D_MODEL), lambda i: (i, 0)),
            _const_spec((1, D_MODEL)),
            _const_spec((D_MODEL, D_MODEL)),
            _const_spec((D_MODEL, D_MODEL)),
        ],
        out_specs=(pl.BlockSpec((rows, D_MODEL), lambda i: (i, 0)),
                   pl.BlockSpec((rows, D_MODEL), lambda i: (i, 0))),
        compiler_params=pltpu.CompilerParams(
            dimension_semantics=("arbitrary",), vmem_limit_bytes=VMEM_LIMIT),
        name="mem_kv",
    )(mem, g, w_k, w_v)


def _attend(q, k, v):
    outs = []
    for hd in range(X_HEADS):
        cols = slice(hd * X_HEAD_DIM, (hd + 1) * X_HEAD_DIM)
        s = _dot_nt(q[:, cols].astype(BF16), k[:, cols]) * (X_HEAD_DIM ** -0.5)
        e = jnp.exp(s - jnp.max(s, axis=-1, keepdims=True))
        p = e / jnp.sum(e, axis=-1, keepdims=True)
        outs.append(_dot(p.astype(BF16), v[:, cols]).astype(BF16))
    return jnp.concatenate(outs, axis=-1)


def _conv_block(cbuf_ref, cw_ref, base, r0, c0):
    nrow = CONV_BLOCK_ROWS
    lanes = slice(c0, c0 + LANES)
    y = None
    for res in range(SUBLANES):
        taps = [k for k in range(CONV_K) if (base + k) % SUBLANES == res]
        ext = nrow + (SUBLANES if res else 0)
        part = None
        for k in taps:
            start = r0 + base + k - res
            term = cbuf_ref[start:start + ext, lanes] * cw_ref[k:k + 1, lanes]
            part = term if part is None else part + term
        part = part[res:res + nrow, :]
        y = part if y is None else y + part
    return y


def _mix_prompt_kernel(h_ref, k_ref, v_ref, gpre_ref, win_ref, slng_ref, slnb_ref, ws_ref, bs_ref,
                       cw_ref, cb_ref, clng_ref, clnb_ref, wout_ref, gpost_ref,
                       xgpre_ref, wq_ref, wo_ref, xgpost_ref,
                       o_ref, cstate_ref, chunkv_ref, cbuf_ref):
    t = pl.program_id(1)
    n_t = pl.num_programs(1)
    rows = MIX_ROWS

    @pl.when(t == 0)
    def _():
        cbuf_ref[0:CARRY_ROWS, :] = jnp.zeros((CARRY_ROWS, CONV_WIDTH), F32)

    h = h_ref[...]
    n = _rms(h, gpre_ref[...]).astype(BF16)
    z = _dot(n, win_ref[...])
    u = z[:, 0:SGU_WIDTH]
    v = _ln(z[:, SGU_WIDTH:2 * SGU_WIDTH], slng_ref[...], slnb_ref[...])
    a = z[:, 2 * SGU_WIDTH:2 * SGU_WIDTH + CONV_WIDTH]
    gate = z[:, 2 * SGU_WIDTH + CONV_WIDTH:]

    @pl.when(t == n_t - 1)
    def _():
        chunkv_ref[...] = v[rows - CHUNK:, :]

    row_id = lax.broadcasted_iota(jnp.int32, (CHUNK, CHUNK), 0)
    col_id = lax.broadcasted_iota(jnp.int32, (CHUNK, CHUNK), 1)
    vb = v.astype(BF16)
    s_heads = []
    for hd in range(SGU_HEADS):
        ws = jnp.where(row_id >= col_id, ws_ref[hd], 0.0).astype(BF16)
        cols = slice(hd * SGU_HEAD_DIM, (hd + 1) * SGU_HEAD_DIM)
        s_chunks = [_dot(ws, vb[c * CHUNK:(c + 1) * CHUNK, cols]) for c in range(rows // CHUNK)]
        s_heads.append(jnp.concatenate(s_chunks, axis=0))
    bias = jnp.concatenate([bs_ref[...]] * (rows // CHUNK), axis=0)
    out_a = u * (jnp.concatenate(s_heads, axis=-1) + bias)

    cbuf_ref[CARRY_ROWS:CARRY_ROWS + rows, :] = a * jax.nn.sigmoid(gate)
    base = CARRY_ROWS - (CONV_K - 1)
    conv = jnp.concatenate(
        [jnp.concatenate([_conv_block(cbuf_ref, cw_ref, base, r0, c0)
                          for c0 in range(0, CONV_WIDTH, LANES)], axis=-1)
         for r0 in range(0, rows, CONV_BLOCK_ROWS)], axis=0) + cb_ref[...]
    out_b = _silu(_ln(conv, clng_ref[...], clnb_ref[...]))

    @pl.when(t == n_t - 1)
    def _():
        cstate_ref[...] = cbuf_ref[rows + base:rows + CARRY_ROWS, :]

    cbuf_ref[0:CARRY_ROWS, :] = cbuf_ref[rows:rows + CARRY_ROWS, :]

    mixed = jnp.concatenate([out_a.astype(BF16), out_b.astype(BF16)], axis=-1)
    h2 = h + _rms(_dot(mixed, wout_ref[...]), gpost_ref[...])

    q = _dot(_rms(h2, xgpre_ref[...]).astype(BF16), wq_ref[...])
    att = _attend(q, k_ref[...].astype(BF16), v_ref[...].astype(BF16))
    o_ref[...] = h2 + _rms(_dot(att, wo_ref[...]), xgpost_ref[...])


def _mix_prompt(h, k, v, p):
    bsz, seq, _ = h.shape
    tok = pl.BlockSpec((None, MIX_ROWS, D_MODEL), lambda b, t: (b, t, 0))
    mem = pl.BlockSpec((None, MEM_TOKENS, D_MODEL), lambda b, t: (b, 0, 0))
    return pl.pallas_call(
        _mix_prompt_kernel,
        out_shape=(jax.ShapeDtypeStruct((bsz, seq, D_MODEL), F32),
                   jax.ShapeDtypeStruct((bsz, CONV_K - 1, CONV_WIDTH), F32),
                   jax.ShapeDtypeStruct((bsz, CHUNK, SGU_WIDTH), F32)),
        grid=(bsz, seq // MIX_ROWS),
        in_specs=[
            tok, mem, mem,
            _const_spec((1, D_MODEL)),
            _const_spec((D_MODEL, 2 * SGU_WIDTH + 2 * CONV_WIDTH)),
            _const_spec((1, SGU_WIDTH)), _const_spec((1, SGU_WIDTH)),
            _const_spec((SGU_HEADS, CHUNK, CHUNK)),
            _const_spec((CHUNK, SGU_WIDTH)),
            _const_spec((CONV_K, CONV_WIDTH)), _const_spec((1, CONV_WIDTH)),
            _const_spec((1, CONV_WIDTH)), _const_spec((1, CONV_WIDTH)),
            _const_spec((D_MODEL, D_MODEL)), _const_spec((1, D_MODEL)),
            _const_spec((1, D_MODEL)),
            _const_spec((D_MODEL, D_MODEL)), _const_spec((D_MODEL, D_MODEL)),
            _const_spec((1, D_MODEL)),
        ],
        out_specs=(tok,
                   pl.BlockSpec((None, CONV_K - 1, CONV_WIDTH), lambda b, t: (b, 0, 0)),
                   pl.BlockSpec((None, CHUNK, SGU_WIDTH), lambda b, t: (b, 0, 0))),
        scratch_shapes=[pltpu.VMEM((MIX_ROWS + CARRY_ROWS, CONV_WIDTH), F32)],
        compiler_params=pltpu.CompilerParams(
            dimension_semantics=("arbitrary", "arbitrary"), vmem_limit_bytes=VMEM_LIMIT),
        name="mix_prompt",
    )(h, k, v, p["g_pre"], p["w_in"], p["sln_g"], p["sln_b"], p["w_s"], p["b_s_full"],
      p["conv_w"], p["conv_b"], p["cln_g"], p["cln_b"], p["w_out"], p["g_post"],
      p["xg_pre"], p["w_q"], p["w_o"], p["xg_post"])


def _cached_head(c_ref, b, hd):
    halves = X_HEAD_DIM // LANES
    stride = X_HEADS * halves
    parts = [c_ref[b, pl.ds(j * X_HEADS + hd, MEM_TOKENS, stride=stride), :] for j in range(halves)]
    return jnp.concatenate(parts, axis=-1).astype(BF16)


def _mix_sample_kernel(dec_seq, h_ref, st_ref, k_ref, v_ref, gpre_ref, win_ref, slng_ref, slnb_ref,
                       wsx_ref, bsx_ref, cwx_ref, cb_ref, clng_ref, clnb_ref, wout_ref, gpost_ref,
                       xgpre_ref, wq_ref, wo_ref, xgpost_ref,
                       o_ref, cstate_ref, chunkv_ref, glu_ref):
    bt = SAMPLE_BATCH_TILE
    hist = CONV_K - 1
    h = h_ref[...]
    n = _rms(h, gpre_ref[...]).astype(BF16)
    z = _dot(n, win_ref[...])
    u = z[:, 0:SGU_WIDTH]
    v = _ln(z[:, SGU_WIDTH:2 * SGU_WIDTH], slng_ref[...], slnb_ref[...])
    a = z[:, 2 * SGU_WIDTH:2 * SGU_WIDTH + CONV_WIDTH]
    gate = z[:, 2 * SGU_WIDTH + CONV_WIDTH:]
    chunkv_ref[...] = v

    v3 = v.reshape(bt, dec_seq, SGU_WIDTH)
    s = jnp.broadcast_to(bsx_ref[...][None], (bt, dec_seq, SGU_WIDTH))
    for j in range(dec_seq):
        s = s + wsx_ref[j][None] * v3[:, j:j + 1, :]
    out_a = u * s.reshape(bt * dec_seq, SGU_WIDTH)

    glu = a * jax.nn.sigmoid(gate)
    glu3 = glu.reshape(bt, dec_seq, CONV_WIDTH)
    new_part = jnp.broadcast_to(cb_ref[...][None], (bt, dec_seq, CONV_WIDTH))
    for r in range(dec_seq):
        new_part = new_part + cwx_ref[hist + r][None] * glu3[:, r:r + 1, :]
    old_parts = []
    for b in range(bt):
        acc = cwx_ref[0] * st_ref[0, b:b + 1, :]
        for r in range(1, hist):
            acc = acc + cwx_ref[r] * st_ref[r, b:b + 1, :]
        old_parts.append(acc)
    conv = new_part.reshape(bt * dec_seq, CONV_WIDTH) + jnp.concatenate(old_parts, axis=0)
    out_b = _silu(_ln(conv, clng_ref[...], clnb_ref[...]))

    cstate_ref[0:hist - dec_seq, :, :] = st_ref[dec_seq:hist, :, :]
    for sl in range(CONV_WIDTH // LANES):
        glu_ref[sl] = glu[:, sl * LANES:(sl + 1) * LANES]
    for tt in range(dec_seq):
        for sl in range(CONV_WIDTH // LANES):
            cstate_ref[hist - dec_seq + tt, :, sl * LANES:(sl + 1) * LANES] = (
                glu_ref[sl, pl.ds(tt, bt, stride=dec_seq), :])

    mixed = jnp.concatenate([out_a.astype(BF16), out_b.astype(BF16)], axis=-1)
    h2 = h + _rms(_dot(mixed, wout_ref[...]), gpost_ref[...])

    q = _dot(_rms(h2, xgpre_ref[...]).astype(BF16), wq_ref[...])
    scores = []
    for b in range(bt):
        for hd in range(X_HEADS):
            qbh = q[b * dec_seq:(b + 1) * dec_seq, hd * X_HEAD_DIM:(hd + 1) * X_HEAD_DIM]
            scores.append(_dot_nt(qbh.astype(BF16), _cached_head(k_ref, b, hd)))
    sc = jnp.concatenate(scores, axis=0) * (X_HEAD_DIM ** -0.5)
    e = jnp.exp(sc - jnp.max(sc, axis=-1, keepdims=True))
    pr = e / jnp.sum(e, axis=-1, keepdims=True)
    atts = []
    for b in range(bt):
        heads = []
        for hd in range(X_HEADS):
            i0 = (b * X_HEADS + hd) * dec_seq
            heads.append(_dot(pr[i0:i0 + dec_seq, :].astype(BF16), _cached_head(v_ref, b, hd)))
        atts.append(jnp.concatenate(heads, axis=-1))
    att = jnp.concatenate(atts, axis=0).astype(BF16)
    o_ref[...] = h2 + _rms(_dot(att, wo_ref[...]), xgpost_ref[...])


def _mix_sample(h, state, k, v, p, dec_seq):
    n_rows = h.shape[0]
    bsz = n_rows // dec_seq
    bt = SAMPLE_BATCH_TILE
    hist = CONV_K - 1
    tok = pl.BlockSpec((bt * dec_seq, D_MODEL), lambda i: (i, 0))
    mem = pl.BlockSpec((bt,) + k.shape[1:], lambda i: (i, 0, 0))
    st = pl.BlockSpec((hist, bt, CONV_WIDTH), lambda i: (0, i, 0))
    return pl.pallas_call(
        functools.partial(_mix_sample_kernel, dec_seq),
        out_shape=(jax.ShapeDtypeStruct((n_rows, D_MODEL), F32),
                   jax.ShapeDtypeStruct((hist, bsz, CONV_WIDTH), F32),
                   jax.ShapeDtypeStruct((n_rows, SGU_WIDTH), F32)),
        grid=(bsz // bt,),
        in_specs=[
            tok, st, mem, mem,
            _const_spec((1, D_MODEL)),
            _const_spec((D_MODEL, 2 * SGU_WIDTH + 2 * CONV_WIDTH)),
            _const_spec((1, SGU_WIDTH)), _const_spec((1, SGU_WIDTH)),
            _const_spec((dec_seq, dec_seq, SGU_WIDTH)),
            _const_spec((dec_seq, SGU_WIDTH)),
            _const_spec((hist + dec_seq, dec_seq, CONV_WIDTH)), _const_spec((1, CONV_WIDTH)),
            _const_spec((1, CONV_WIDTH)), _const_spec((1, CONV_WIDTH)),
            _const_spec((D_MODEL, D_MODEL)), _const_spec((1, D_MODEL)),
            _const_spec((1, D_MODEL)),
            _const_spec((D_MODEL, D_MODEL)), _const_spec((D_MODEL, D_MODEL)),
            _const_spec((1, D_MODEL)),
        ],
        out_specs=(tok, st, pl.BlockSpec((bt * dec_seq, SGU_WIDTH), lambda i: (i, 0))),
        scratch_shapes=[pltpu.VMEM((CONV_WIDTH // LANES, bt * dec_seq, LANES), F32)],
        compiler_params=pltpu.CompilerParams(
            dimension_semantics=("arbitrary",), vmem_limit_bytes=VMEM_LIMIT),
        name="mix_sample",
    )(h, state, k, v, p["g_pre"], p["w_in"], p["sln_g"], p["sln_b"], p["w_s_rows"], p["b_s_rows"],
      p["conv_w_rows"], p["conv_b"], p["cln_g"], p["cln_b"], p["w_out"], p["g_post"],
      p["xg_pre"], p["w_q"], p["w_o"], p["xg_post"])


def _head_tiled_rows(c):
    b, m, hh, dh = c.shape
    c = c.reshape(b, m, hh, dh // LANES, LANES)
    return jnp.transpose(c, (0, 1, 3, 2, 4)).reshape(b, m * hh * (dh // LANES), LANES)


def _row(x):
    return x.reshape(1, -1)


def kernel(x_prompt, x_sample, mem_prompt, cache_mem_k, cache_mem_v, state_conv, ffn1_g_pre, ffn1_w_gate, ffn1_w_up, ffn1_w_down, ffn1_g_post, mix_g_pre, mix_w_in, sgu_ln_g, sgu_ln_b, sgu_w_s, sgu_b_s, conv_w, conv_b, conv_ln_g, conv_ln_b, mix_w_out, mix_g_post, mem_g, xattn_g_pre, xattn_w_q, xattn_w_k, xattn_w_v, xattn_w_o, xattn_g_post, ffn2_g_pre, ffn2_w_gate, ffn2_w_up, ffn2_w_down, ffn2_g_post):
    depth = ffn1_g_pre.shape[0]
    bsz, seq, _ = x_prompt.shape
    dbsz, dec_seq, _ = x_sample.shape
    hist = CONV_K - 1
    assert seq % MIX_ROWS == 0 and dec_seq <= CHUNK and dec_seq <= hist

    hp = x_prompt.reshape(bsz * seq, D_MODEL)
    hs = x_sample.reshape(dbsz * dec_seq, D_MODEL)
    mem = mem_prompt.reshape(bsz * MEM_TOKENS, D_MODEL)
    outs = [[] for _ in range(6)]
    for l in range(depth):
        ffn1 = (_row(ffn1_g_pre[l]), ffn1_w_gate[l].astype(BF16), ffn1_w_up[l].astype(BF16),
                ffn1_w_down[l].astype(BF16), _row(ffn1_g_post[l]))
        ffn2 = (_row(ffn2_g_pre[l]), ffn2_w_gate[l].astype(BF16), ffn2_w_up[l].astype(BF16),
                ffn2_w_down[l].astype(BF16), _row(ffn2_g_post[l]))
        w_s, b_s, cw = sgu_w_s[l], sgu_b_s[l], conv_w[l]
        tril = jnp.tril(w_s[:, :dec_seq, :dec_seq])
        w_s_rows = jnp.repeat(jnp.transpose(tril, (2, 1, 0)), SGU_HEAD_DIM, axis=2)
        b_s_rows = jnp.repeat(b_s[:, :dec_seq].T, SGU_HEAD_DIM, axis=1)
        tap = jnp.arange(hist + dec_seq)[:, None] - jnp.arange(dec_seq)[None, :]
        conv_w_rows = jnp.where(((tap >= 0) & (tap < CONV_K))[:, :, None],
                                cw[jnp.clip(tap, 0, CONV_K - 1)], 0.0)
        p = dict(
            g_pre=_row(mix_g_pre[l]), w_in=mix_w_in[l].astype(BF16),
            sln_g=_row(sgu_ln_g[l]), sln_b=_row(sgu_ln_b[l]),
            w_s=w_s, b_s_full=jnp.repeat(b_s.T, SGU_HEAD_DIM, axis=1),
            w_s_rows=w_s_rows, b_s_rows=b_s_rows,
            conv_w=cw, conv_w_rows=conv_w_rows, conv_b=_row(conv_b[l]),
            cln_g=_row(conv_ln_g[l]), cln_b=_row(conv_ln_b[l]),
            w_out=mix_w_out[l].astype(BF16), g_post=_row(mix_g_post[l]),
            xg_pre=_row(xattn_g_pre[l]), w_q=xattn_w_q[l].astype(BF16),
            w_o=xattn_w_o[l].astype(BF16), xg_post=_row(xattn_g_post[l]),
        )

        k_p, v_p = _mem_kv(mem, _row(mem_g[l]), xattn_w_k[l].astype(BF16), xattn_w_v[l].astype(BF16))

        hp = _ffn(hp, *ffn1)
        hp, cs_p, cv_p = _mix_prompt(hp.reshape(bsz, seq, D_MODEL),
                                     k_p.reshape(bsz, MEM_TOKENS, D_MODEL),
                                     v_p.reshape(bsz, MEM_TOKENS, D_MODEL), p)
        hp = _ffn(hp.reshape(bsz * seq, D_MODEL), *ffn2)

        hs = _ffn(hs, *ffn1)
        hs, cs_s, cv_s = _mix_sample(hs, jnp.transpose(state_conv[l], (1, 0, 2)),
                                     _head_tiled_rows(cache_mem_k[l]),
                                     _head_tiled_rows(cache_mem_v[l]), p, dec_seq)
        hs = _ffn(hs, *ffn2)

        outs[0].append(k_p.reshape(bsz, MEM_TOKENS, X_HEADS, X_HEAD_DIM))
        outs[1].append(v_p.reshape(bsz, MEM_TOKENS, X_HEADS, X_HEAD_DIM))
        outs[2].append(cs_p)
        outs[3].append(jnp.transpose(cs_s, (1, 0, 2)))
        outs[4].append(cv_p)
        outs[5].append(cv_s.reshape(dbsz, dec_seq, SGU_WIDTH))

    return (hp.reshape(bsz, seq, D_MODEL), hs.reshape(dbsz, dec_seq, D_MODEL),
            jnp.stack(outs[0]), jnp.stack(outs[1]), jnp.stack(outs[2]), jnp.stack(outs[3]),
            jnp.stack(outs[4]), jnp.stack(outs[5]))
```

```python
import functools

import jax
import jax.numpy as jnp
from jax import lax
from jax.experimental import pallas as pl
from jax.experimental.pallas import tpu as pltpu

D_MODEL = 1024
SGU_WIDTH = 512
CONV_WIDTH = 512
SGU_HEADS = 4
SGU_HEAD_DIM = 128
CHUNK = 128
CONV_K = 31
MEM_TOKENS = 256
X_HEADS = 4
X_HEAD_DIM = 256
FFN_DIM = 4096
EPS = 1e-6
LANES = 128
SUBLANES = 8

F32 = jnp.float32
BF16 = jnp.bfloat16

FFN_ROWS = 512
FFN_COLS = 512
MIX_ROWS = 512
SAMPLE_BATCH_TILE = 8
CARRY_ROWS = 32
CONV_BLOCK_ROWS = 128
VMEM_LIMIT = 56 * 1024 * 1024


def _rms(x, g):
    return x * lax.rsqrt(jnp.mean(x * x, axis=-1, keepdims=True) + EPS) * g


def _ln(x, g, b):
    mu = jnp.mean(x, axis=-1, keepdims=True)
    xc = x - mu
    return xc * lax.rsqrt(jnp.mean(xc * xc, axis=-1, keepdims=True) + EPS) * g + b


def _silu(x):
    return x * jax.nn.sigmoid(x)


def _dot(a, b):
    return jnp.dot(a, b, preferred_element_type=F32)


def _dot_nt(a, b):
    return lax.dot_general(a, b, (((1,), (1,)), ((), ())), preferred_element_type=F32)


def _const_spec(shape):
    nd = len(shape)
    return pl.BlockSpec(shape, lambda *_: (0,) * nd, pipeline_mode=pl.Buffered(1))


def _ffn_tile(x_ref, gpre_ref, wg_ref, wu_ref, wd_ref, gpost_ref, o_ref, acc_ref):
    x = x_ref[...]
    n = _rms(x, gpre_ref[...]).astype(BF16)
    for c in range(FFN_DIM // FFN_COLS):
        cols = slice(c * FFN_COLS, (c + 1) * FFN_COLS)
        g = _dot(n, wg_ref[:, cols])
        u = _dot(n, wu_ref[:, cols])
        h = (_silu(g) * u).astype(BF16)
        d = _dot(h, wd_ref[cols, :])
        if c == 0:
            acc_ref[...] = d
        else:
            acc_ref[...] += d
    o_ref[...] = x + 0.5 * _rms(acc_ref[...], gpost_ref[...])


def _ffn_kernel(n_a, xa_ref, xb_ref, gpre_ref, wg_ref, wu_ref, wd_ref, gpost_ref, oa_ref, ob_ref, acc_ref):
    i = pl.program_id(0)

    @pl.when(i < n_a)
    def _():
        _ffn_tile(xa_ref, gpre_ref, wg_ref, wu_ref, wd_ref, gpost_ref, oa_ref, acc_ref)

    @pl.when(i >= n_a)
    def _():
        _ffn_tile(xb_ref, gpre_ref, wg_ref, wu_ref, wd_ref, gpost_ref, ob_ref, acc_ref)


def _ffn(xa, xb, g_pre, w_gate, w_up, w_down, g_post):
    n_a = xa.shape[0] // FFN_ROWS
    n_b = xb.shape[0] // FFN_ROWS
    spec_a = pl.BlockSpec((FFN_ROWS, D_MODEL), lambda i: (jnp.minimum(i, n_a - 1), 0))
    spec_b = pl.BlockSpec((FFN_ROWS, D_MODEL), lambda i: (jnp.maximum(i - n_a, 0), 0))
    return pl.pallas_call(
        functools.partial(_ffn_kernel, n_a),
        out_shape=(jax.ShapeDtypeStruct(xa.shape, F32), jax.ShapeDtypeStruct(xb.shape, F32)),
        grid=(n_a + n_b,),
        in_specs=[
            spec_a, spec_b,
            _const_spec((1, D_MODEL)),
            _const_spec((D_MODEL, FFN_DIM)),
            _const_spec((D_MODEL, FFN_DIM)),
            _const_spec((FFN_DIM, D_MODEL)),
            _const_spec((1, D_MODEL)),
        ],
        out_specs=(spec_a, spec_b),
        scratch_shapes=[pltpu.VMEM((FFN_ROWS, D_MODEL), F32)],
        compiler_params=pltpu.CompilerParams(
            dimension_semantics=("arbitrary",), vmem_limit_bytes=VMEM_LIMIT),
        name="ffn",
    )(xa, xb, g_pre, w_gate, w_up, w_down, g_post)


def _mem_kv_kernel(m_ref, g_ref, wk_ref, wv_ref, kt_ref, vt_ref, kb_ref, vb_ref):
    mn = _rms(m_ref[...], g_ref[...]).astype(BF16)
    halves = X_HEAD_DIM // LANES
    for w_ref, t_ref, b_ref in ((wk_ref, kt_ref, kb_ref), (wv_ref, vt_ref, vb_ref)):
        x = _dot(mn, w_ref[...])
        b_ref[...] = x.astype(BF16)
        for hd in range(X_HEADS):
            for j in range(halves):
                c0 = hd * X_HEAD_DIM + j * LANES
                t_ref[pl.ds(j * X_HEADS + hd, MEM_TOKENS, stride=X_HEADS * halves), :] = x[:, c0:c0 + LANES]


def _mem_kv(mem, g, w_k, w_v):
    bsz = mem.shape[0] // MEM_TOKENS
    tiled_rows = MEM_TOKENS * D_MODEL // LANES
    tiled = jax.ShapeDtypeStruct((bsz, tiled_rows, LANES), F32)
    dense = jax.ShapeDtypeStruct((bsz, MEM_TOKENS, D_MODEL), BF16)
    tiled_spec = pl.BlockSpec((None, tiled_rows, LANES), lambda i: (i, 0, 0))
    dense_spec = pl.BlockSpec((None, MEM_TOKENS, D_MODEL), lambda i: (i, 0, 0))
    return pl.pallas_call(
        _mem_kv_kernel,
        out_shape=(tiled, tiled, dense, dense),
        grid=(bsz,),
        in_specs=[
            pl.BlockSpec((MEM_TOKENS, D_MODEL), lambda i: (i, 0)),
            _const_spec((1, D_MODEL)),
            _const_spec((D_MODEL, D_MODEL)),
            _const_spec((D_MODEL, D_MODEL)),
        ],
        out_specs=(tiled_spec, tiled_spec, dense_spec, dense_spec),
        compiler_params=pltpu.CompilerParams(
            dimension_semantics=("arbitrary",), vmem_limit_bytes=VMEM_LIMIT),
        name="mem_kv",
    )(mem, g, w_k, w_v)


def _attend(q, k, v):
    outs = []
    for hd in range(X_HEADS):
        cols = slice(hd * X_HEAD_DIM, (hd + 1) * X_HEAD_DIM)
        s = _dot_nt(q[:, cols].astype(BF16), k[:, cols]) * (X_HEAD_DIM ** -0.5)
        e = jnp.exp(s - jnp.max(s, axis=-1, keepdims=True))
        p = e / jnp.sum(e, axis=-1, keepdims=True)
        outs.append(_dot(p.astype(BF16), v[:, cols]).astype(BF16))
    return jnp.concatenate(outs, axis=-1)


def _conv_block(cbuf_ref, cw_ref, base, r0, c0):
    nrow = CONV_BLOCK_ROWS
    lanes = slice(c0, c0 + LANES)
    y = None
    for res in range(SUBLANES):
        taps = [k for k in range(CONV_K) if (base + k) % SUBLANES == res]
        ext = nrow + (SUBLANES if res else 0)
        part = None
        for k in taps:
            start = r0 + base + k - res
            term = cbuf_ref[start:start + ext, lanes] * cw_ref[k:k + 1, lanes]
            part = term if part is None else part + term
        part = part[res:res + nrow, :]
        y = part if y is None else y + part
    return y


def _mix_prompt_kernel(tiles_per_seq, h_ref, hprev_ref, k_ref, v_ref, gpre_ref, win_ref, slng_ref, slnb_ref,
                       ws_ref, bs_ref, cw_ref, cb_ref, clng_ref, clnb_ref, wout_ref, gpost_ref,
                       xgpre_ref, wq_ref, wo_ref, xgpost_ref,
                       o_ref, cstate_ref, chunkv_ref, cbuf_ref, mixed_ref, res_ref):
    s = pl.program_id(0)
    rows = MIX_ROWS
    tile = jnp.minimum(s, pl.num_programs(0) - 2)
    last_of_seq = (tile % tiles_per_seq) == tiles_per_seq - 1

    @pl.when(s == 0)
    def _():
        cbuf_ref[0:CARRY_ROWS, :] = jnp.zeros((CARRY_ROWS, CONV_WIDTH), F32)
        mixed_ref[...] = jnp.zeros((rows, D_MODEL), BF16)

    res_ref[...] = hprev_ref[...] + _rms(_dot(mixed_ref[...], wout_ref[...]), gpost_ref[...])

    n = _rms(h_ref[...], gpre_ref[...]).astype(BF16)
    z = _dot(n, win_ref[...])
    u = z[:, 0:SGU_WIDTH]
    v = _ln(z[:, SGU_WIDTH:2 * SGU_WIDTH], slng_ref[...], slnb_ref[...])
    a = z[:, 2 * SGU_WIDTH:2 * SGU_WIDTH + CONV_WIDTH]
    gate = z[:, 2 * SGU_WIDTH + CONV_WIDTH:]
    chunkv_ref[...] = v[rows - CHUNK:, :]

    q = _dot(_rms(res_ref[...], xgpre_ref[...]).astype(BF16), wq_ref[...])

    row_id = lax.broadcasted_iota(jnp.int32, (CHUNK, CHUNK), 0)
    col_id = lax.broadcasted_iota(jnp.int32, (CHUNK, CHUNK), 1)
    vb = v.astype(BF16)
    s_heads = []
    for hd in range(SGU_HEADS):
        ws = jnp.where(row_id >= col_id, ws_ref[hd], 0.0).astype(BF16)
        cols = slice(hd * SGU_HEAD_DIM, (hd + 1) * SGU_HEAD_DIM)
        s_chunks = [_dot(ws, vb[c * CHUNK:(c + 1) * CHUNK, cols]) for c in range(rows // CHUNK)]
        s_heads.append(jnp.concatenate(s_chunks, axis=0))
    bias = jnp.concatenate([bs_ref[...]] * (rows // CHUNK), axis=0)
    out_a = u * (jnp.concatenate(s_heads, axis=-1) + bias)

    att = _attend(q, k_ref[...], v_ref[...])
    o_ref[...] = res_ref[...] + _rms(_dot(att, wo_ref[...]), xgpost_ref[...])

    cbuf_ref[CARRY_ROWS:CARRY_ROWS + rows, :] = a * jax.nn.sigmoid(gate)
    base = CARRY_ROWS - (CONV_K - 1)
    conv = jnp.concatenate(
        [jnp.concatenate([_conv_block(cbuf_ref, cw_ref, base, r0, c0)
                          for c0 in range(0, CONV_WIDTH, LANES)], axis=-1)
         for r0 in range(0, rows, CONV_BLOCK_ROWS)], axis=0) + cb_ref[...]
    out_b = _silu(_ln(conv, clng_ref[...], clnb_ref[...]))
    cstate_ref[...] = cbuf_ref[rows + base:rows + CARRY_ROWS, :]
    cbuf_ref[0:CARRY_ROWS, :] = jnp.where(last_of_seq, 0.0, cbuf_ref[rows:rows + CARRY_ROWS, :])
    mixed_ref[...] = jnp.concatenate([out_a.astype(BF16), out_b.astype(BF16)], axis=-1)


def _mix_prompt(h, k, v, p, tiles_per_seq):
    n_tiles = h.shape[0] // MIX_ROWS
    bsz = n_tiles // tiles_per_seq
    mix_tile = lambda s: jnp.minimum(s, n_tiles - 1)
    att_tile = lambda s: jnp.maximum(s - 1, 0)
    mem = pl.BlockSpec((None, MEM_TOKENS, D_MODEL), lambda s: (att_tile(s) // tiles_per_seq, 0, 0))
    return pl.pallas_call(
        functools.partial(_mix_prompt_kernel, tiles_per_seq),
        out_shape=(jax.ShapeDtypeStruct(h.shape, F32),
                   jax.ShapeDtypeStruct((bsz, CONV_K - 1, CONV_WIDTH), F32),
                   jax.ShapeDtypeStruct((bsz, CHUNK, SGU_WIDTH), F32)),
        grid=(n_tiles + 1,),
        in_specs=[
            pl.BlockSpec((MIX_ROWS, D_MODEL), lambda s: (mix_tile(s), 0)),
            pl.BlockSpec((MIX_ROWS, D_MODEL), lambda s: (att_tile(s), 0)), mem, mem,
            _const_spec((1, D_MODEL)),
            _const_spec((D_MODEL, 2 * SGU_WIDTH + 2 * CONV_WIDTH)),
            _const_spec((1, SGU_WIDTH)), _const_spec((1, SGU_WIDTH)),
            _const_spec((SGU_HEADS, CHUNK, CHUNK)),
            _const_spec((CHUNK, SGU_WIDTH)),
            _const_spec((CONV_K, CONV_WIDTH)), _const_spec((1, CONV_WIDTH)),
            _const_spec((1, CONV_WIDTH)), _const_spec((1, CONV_WIDTH)),
            _const_spec((D_MODEL, D_MODEL)), _const_spec((1, D_MODEL)),
            _const_spec((1, D_MODEL)),
            _const_spec((D_MODEL, D_MODEL)), _const_spec((D_MODEL, D_MODEL)),
            _const_spec((1, D_MODEL)),
        ],
        out_specs=(pl.BlockSpec((MIX_ROWS, D_MODEL), lambda s: (att_tile(s), 0)),
                   pl.BlockSpec((None, CONV_K - 1, CONV_WIDTH),
                                lambda s: (mix_tile(s) // tiles_per_seq, 0, 0)),
                   pl.BlockSpec((None, CHUNK, SGU_WIDTH),
                                lambda s: (mix_tile(s) // tiles_per_seq, 0, 0))),
        scratch_shapes=[pltpu.VMEM((MIX_ROWS + CARRY_ROWS, CONV_WIDTH), F32),
                        pltpu.VMEM((MIX_ROWS, D_MODEL), BF16),
                        pltpu.VMEM((MIX_ROWS, D_MODEL), F32)],
        compiler_params=pltpu.CompilerParams(
            dimension_semantics=("arbitrary",), vmem_limit_bytes=VMEM_LIMIT),
        name="mix_prompt",
    )(h, h, k, v, p["g_pre"], p["w_in"], p["sln_g"], p["sln_b"], p["w_s"], p["b_s_full"],
      p["conv_w"], p["conv_b"], p["cln_g"], p["cln_b"], p["w_out"], p["g_post"],
      p["xg_pre"], p["w_q"], p["w_o"], p["xg_post"])


def _cached_head(c_ref, b, hd):
    halves = X_HEAD_DIM // LANES
    stride = X_HEADS * halves
    parts = [c_ref[b, pl.ds(j * X_HEADS + hd, MEM_TOKENS, stride=stride), :] for j in range(halves)]
    return jnp.concatenate(parts, axis=-1).astype(BF16)


def _mix_sample_kernel(dec_seq, h_ref, st_ref, k_ref, v_ref, gpre_ref, win_ref, slng_ref, slnb_ref,
                       wsx_ref, bsx_ref, cwx_ref, cb_ref, clng_ref, clnb_ref, wout_ref, gpost_ref,
                       xgpre_ref, wq_ref, wo_ref, xgpost_ref,
                       o_ref, cstate_ref, chunkv_ref, glu_ref):
    bt = SAMPLE_BATCH_TILE
    hist = CONV_K - 1
    h = h_ref[...]
    n = _rms(h, gpre_ref[...]).astype(BF16)
    z = _dot(n, win_ref[...])
    u = z[:, 0:SGU_WIDTH]
    v = _ln(z[:, SGU_WIDTH:2 * SGU_WIDTH], slng_ref[...], slnb_ref[...])
    a = z[:, 2 * SGU_WIDTH:2 * SGU_WIDTH + CONV_WIDTH]
    gate = z[:, 2 * SGU_WIDTH + CONV_WIDTH:]
    chunkv_ref[...] = v

    v3 = v.reshape(bt, dec_seq, SGU_WIDTH)
    s = jnp.broadcast_to(bsx_ref[...][None], (bt, dec_seq, SGU_WIDTH))
    for j in range(dec_seq):
        s = s + wsx_ref[j][None] * v3[:, j:j + 1, :]
    out_a = u * s.reshape(bt * dec_seq, SGU_WIDTH)

    glu = a * jax.nn.sigmoid(gate)
    glu3 = glu.reshape(bt, dec_seq, CONV_WIDTH)
    new_part = jnp.broadcast_to(cb_ref[...][None], (bt, dec_seq, CONV_WIDTH))
    for r in range(dec_seq):
        new_part = new_part + cwx_ref[hist + r][None] * glu3[:, r:r + 1, :]
    old_parts = []
    for b in range(bt):
        acc = cwx_ref[0] * st_ref[0, b:b + 1, :]
        for r in range(1, hist):
            acc = acc + cwx_ref[r] * st_ref[r, b:b + 1, :]
        old_parts.append(acc)
    conv = new_part.reshape(bt * dec_seq, CONV_WIDTH) + jnp.concatenate(old_parts, axis=0)
    out_b = _silu(_ln(conv, clng_ref[...], clnb_ref[...]))

    cstate_ref[0:hist - dec_seq, :, :] = st_ref[dec_seq:hist, :, :]
    for sl in range(CONV_WIDTH // LANES):
        glu_ref[sl] = glu[:, sl * LANES:(sl + 1) * LANES]
    for tt in range(dec_seq):
        for sl in range(CONV_WIDTH // LANES):
            cstate_ref[hist - dec_seq + tt, :, sl * LANES:(sl + 1) * LANES] = (
                glu_ref[sl, pl.ds(tt, bt, stride=dec_seq), :])

    mixed = jnp.concatenate([out_a.astype(BF16), out_b.astype(BF16)], axis=-1)
    h2 = h + _rms(_dot(mixed, wout_ref[...]), gpost_ref[...])

    q = _dot(_rms(h2, xgpre_ref[...]).astype(BF16), wq_ref[...])
    scores = []
    for b in range(bt):
        for hd in range(X_HEADS):
            qbh = q[b * dec_seq:(b + 1) * dec_seq, hd * X_HEAD_DIM:(hd + 1) * X_HEAD_DIM]
            scores.append(_dot_nt(qbh.astype(BF16), _cached_head(k_ref, b, hd)))
    sc = jnp.concatenate(scores, axis=0) * (X_HEAD_DIM ** -0.5)
    e = jnp.exp(sc - jnp.max(sc, axis=-1, keepdims=True))
    pr = e / jnp.sum(e, axis=-1, keepdims=True)
    atts = []
    for b in range(bt):
        heads = []
        for hd in range(X_HEADS):
            i0 = (b * X_HEADS + hd) * dec_seq
            heads.append(_dot(pr[i0:i0 + dec_seq, :].astype(BF16), _cached_head(v_ref, b, hd)))
        atts.append(jnp.concatenate(heads, axis=-1))
    att = jnp.concatenate(atts, axis=0).astype(BF16)
    o_ref[...] = h2 + _rms(_dot(att, wo_ref[...]), xgpost_ref[...])


def _mix_sample(h, state, k, v, p, dec_seq):
    n_rows = h.shape[0]
    bsz = n_rows // dec_seq
    bt = SAMPLE_BATCH_TILE
    hist = CONV_K - 1
    tok = pl.BlockSpec((bt * dec_seq, D_MODEL), lambda i: (i, 0))
    mem = pl.BlockSpec((bt,) + k.shape[1:], lambda i: (i, 0, 0))
    st = pl.BlockSpec((hist, bt, CONV_WIDTH), lambda i: (0, i, 0))
    return pl.pallas_call(
        functools.partial(_mix_sample_kernel, dec_seq),
        out_shape=(jax.ShapeDtypeStruct((n_rows, D_MODEL), F32),
                   jax.ShapeDtypeStruct((hist, bsz, CONV_WIDTH), F32),
                   jax.ShapeDtypeStruct((n_rows, SGU_WIDTH), F32)),
        grid=(bsz // bt,),
        in_specs=[
            tok, st, mem, mem,
            _const_spec((1, D_MODEL)),
            _const_spec((D_MODEL, 2 * SGU_WIDTH + 2 * CONV_WIDTH)),
            _const_spec((1, SGU_WIDTH)), _const_spec((1, SGU_WIDTH)),
            _const_spec((dec_seq, dec_seq, SGU_WIDTH)),
            _const_spec((dec_seq, SGU_WIDTH)),
            _const_spec((hist + dec_seq, dec_seq, CONV_WIDTH)), _const_spec((1, CONV_WIDTH)),
            _const_spec((1, CONV_WIDTH)), _const_spec((1, CONV_WIDTH)),
            _const_spec((D_MODEL, D_MODEL)), _const_spec((1, D_MODEL)),
            _const_spec((1, D_MODEL)),
            _const_spec((D_MODEL, D_MODEL)), _const_spec((D_MODEL, D_MODEL)),
            _const_spec((1, D_MODEL)),
        ],
        out_specs=(tok, st, pl.BlockSpec((bt * dec_seq, SGU_WIDTH), lambda i: (i, 0))),
        scratch_shapes=[pltpu.VMEM((CONV_WIDTH // LANES, bt * dec_seq, LANES), F32)],
        compiler_params=pltpu.CompilerParams(
            dimension_semantics=("arbitrary",), vmem_limit_bytes=VMEM_LIMIT),
        name="mix_sample",
    )(h, state, k, v, p["g_pre"], p["w_in"], p["sln_g"], p["sln_b"], p["w_s_rows"], p["b_s_rows"],
      p["conv_w_rows"], p["conv_b"], p["cln_g"], p["cln_b"], p["w_out"], p["g_post"],
      p["xg_pre"], p["w_q"], p["w_o"], p["xg_post"])


def _head_tiled_rows(c):
    b, m, hh, dh = c.shape
    c = c.reshape(b, m, hh, dh // LANES, LANES)
    return jnp.transpose(c, (0, 1, 3, 2, 4)).reshape(b, m * hh * (dh // LANES), LANES)


def _from_head_tiled_rows(t):
    b = t.shape[0]
    halves = X_HEAD_DIM // LANES
    t = t.reshape(b, MEM_TOKENS, halves, X_HEADS, LANES)
    return jnp.transpose(t, (0, 1, 3, 2, 4)).reshape(b, MEM_TOKENS, X_HEADS, X_HEAD_DIM)


def _row(x):
    return x.reshape(1, -1)


def kernel(x_prompt, x_sample, mem_prompt, cache_mem_k, cache_mem_v, state_conv, ffn1_g_pre, ffn1_w_gate, ffn1_w_up, ffn1_w_down, ffn1_g_post, mix_g_pre, mix_w_in, sgu_ln_g, sgu_ln_b, sgu_w_s, sgu_b_s, conv_w, conv_b, conv_ln_g, conv_ln_b, mix_w_out, mix_g_post, mem_g, xattn_g_pre, xattn_w_q, xattn_w_k, xattn_w_v, xattn_w_o, xattn_g_post, ffn2_g_pre, ffn2_w_gate, ffn2_w_up, ffn2_w_down, ffn2_g_post):
    depth = ffn1_g_pre.shape[0]
    bsz, seq, _ = x_prompt.shape
    dbsz, dec_seq, _ = x_sample.shape
    hist = CONV_K - 1
    assert seq % MIX_ROWS == 0 and dec_seq <= CHUNK and dec_seq <= hist

    hp = x_prompt.reshape(bsz * seq, D_MODEL)
    hs = x_sample.reshape(dbsz * dec_seq, D_MODEL)
    mem = mem_prompt.reshape(bsz * MEM_TOKENS, D_MODEL)
    outs = [[] for _ in range(6)]
    for l in range(depth):
        ffn1 = (_row(ffn1_g_pre[l]), ffn1_w_gate[l].astype(BF16), ffn1_w_up[l].astype(BF16),
                ffn1_w_down[l].astype(BF16), _row(ffn1_g_post[l]))
        ffn2 = (_row(ffn2_g_pre[l]), ffn2_w_gate[l].astype(BF16), ffn2_w_up[l].astype(BF16),
                ffn2_w_down[l].astype(BF16), _row(ffn2_g_post[l]))
        w_s, b_s, cw = sgu_w_s[l], sgu_b_s[l], conv_w[l]
        tril = jnp.tril(w_s[:, :dec_seq, :dec_seq])
        w_s_rows = jnp.repeat(jnp.transpose(tril, (2, 1, 0)), SGU_HEAD_DIM, axis=2)
        b_s_rows = jnp.repeat(b_s[:, :dec_seq].T, SGU_HEAD_DIM, axis=1)
        tap = jnp.arange(hist + dec_seq)[:, None] - jnp.arange(dec_seq)[None, :]
        conv_w_rows = jnp.where(((tap >= 0) & (tap < CONV_K))[:, :, None],
                                cw[jnp.clip(tap, 0, CONV_K - 1)], 0.0)
        p = dict(
            g_pre=_row(mix_g_pre[l]), w_in=mix_w_in[l].astype(BF16),
            sln_g=_row(sgu_ln_g[l]), sln_b=_row(sgu_ln_b[l]),
            w_s=w_s, b_s_full=jnp.repeat(b_s.T, SGU_HEAD_DIM, axis=1),
            w_s_rows=w_s_rows, b_s_rows=b_s_rows,
            conv_w=cw, conv_w_rows=conv_w_rows, conv_b=_row(conv_b[l]),
            cln_g=_row(conv_ln_g[l]), cln_b=_row(conv_ln_b[l]),
            w_out=mix_w_out[l].astype(BF16), g_post=_row(mix_g_post[l]),
            xg_pre=_row(xattn_g_pre[l]), w_q=xattn_w_q[l].astype(BF16),
            w_o=xattn_w_o[l].astype(BF16), xg_post=_row(xattn_g_post[l]),
        )

        k_t, v_t, k_b, v_b = _mem_kv(mem, _row(mem_g[l]), xattn_w_k[l].astype(BF16),
                                     xattn_w_v[l].astype(BF16))

        hp, hs = _ffn(hp, hs, *ffn1)
        hp, cs_p, cv_p = _mix_prompt(hp, k_b, v_b, p, seq // MIX_ROWS)
        hs, cs_s, cv_s = _mix_sample(hs, jnp.transpose(state_conv[l], (1, 0, 2)),
                                     _head_tiled_rows(cache_mem_k[l]),
                                     _head_tiled_rows(cache_mem_v[l]), p, dec_seq)
        hp, hs = _ffn(hp, hs, *ffn2)

        outs[0].append(_from_head_tiled_rows(k_t))
        outs[1].append(_from_head_tiled_rows(v_t))
        outs[2].append(cs_p)
        outs[3].append(jnp.transpose(cs_s, (1, 0, 2)))
        outs[4].append(cv_p)
        outs[5].append(cv_s.reshape(dbsz, dec_seq, SGU_WIDTH))

    return (hp.reshape(bsz, seq, D_MODEL), hs.reshape(dbsz, dec_seq, D_MODEL),
            jnp.stack(outs[0]), jnp.stack(outs[1]), jnp.stack(outs[2]), jnp.stack(outs[3]),
            jnp.stack(outs[4]), jnp.stack(outs[5]))
```

```python
import functools

import jax
import jax.numpy as jnp
from jax import lax
from jax.experimental import pallas as pl
from jax.experimental.pallas import tpu as pltpu

D_MODEL = 1024
SGU_WIDTH = 512
CONV_WIDTH = 512
SGU_HEADS = 4
SGU_HEAD_DIM = 128
CHUNK = 128
CONV_K = 31
MEM_TOKENS = 256
X_HEADS = 4
X_HEAD_DIM = 256
FFN_DIM = 4096
EPS = 1e-6
LANES = 128
SUBLANES = 8

F32 = jnp.float32
BF16 = jnp.bfloat16

FFN_ROWS = 512
FFN_COLS = 512
MIX_ROWS = 512
SAMPLE_BATCH_TILE = 8
CARRY_ROWS = 32
CONV_BLOCK_ROWS = 128
VMEM_LIMIT = 56 * 1024 * 1024
FFN_VMEM_LIMIT = 60 * 1024 * 1024
CAST_SLABS = 32
BF16_SUBLANES = 16


def _rms(x, g):
    return x * lax.rsqrt(jnp.mean(x * x, axis=-1, keepdims=True) + EPS) * g


def _ln(x, g, b):
    mu = jnp.mean(x, axis=-1, keepdims=True)
    xc = x - mu
    return xc * lax.rsqrt(jnp.mean(xc * xc, axis=-1, keepdims=True) + EPS) * g + b


def _silu(x):
    return x * jax.nn.sigmoid(x)


def _dot(a, b):
    return jnp.dot(a, b, preferred_element_type=F32)


def _dot_nt(a, b):
    return lax.dot_general(a, b, (((1,), (1,)), ((), ())), preferred_element_type=F32)


def _const_spec(shape):
    nd = len(shape)
    return pl.BlockSpec(shape, lambda *_: (0,) * nd, pipeline_mode=pl.Buffered(1))


def _ffn_tile(x_ref, gpre_ref, wg_ref, wu_ref, wd_ref, gpost_ref, o_ref, acc_ref):
    x = x_ref[...]
    n = _rms(x, gpre_ref[...]).astype(BF16)
    for c in range(FFN_DIM // FFN_COLS):
        cols = slice(c * FFN_COLS, (c + 1) * FFN_COLS)
        g = _dot(n, wg_ref[:, cols])
        u = _dot(n, wu_ref[:, cols])
        h = (_silu(g) * u).astype(BF16)
        d = _dot(h, wd_ref[cols, :])
        if c == 0:
            acc_ref[...] = d
        else:
            acc_ref[...] += d
    o_ref[...] = x + 0.5 * _rms(acc_ref[...], gpost_ref[...])


def _ffn_kernel(n_a, n_cast, xa_ref, xb_ref, gpre_ref, wg_ref, wu_ref, wd_ref, gpost_ref, *refs):
    cast_src = refs[:n_cast]
    oa_ref, ob_ref = refs[n_cast:n_cast + 2]
    cast_dst = refs[n_cast + 2:2 * n_cast + 2]
    acc_ref = refs[2 * n_cast + 2]
    i = pl.program_id(0)

    for src, dst in zip(cast_src, cast_dst):
        dst[...] = src[...].astype(BF16)

    @pl.when(i < n_a)
    def _():
        _ffn_tile(xa_ref, gpre_ref, wg_ref, wu_ref, wd_ref, gpost_ref, oa_ref, acc_ref)

    @pl.when(i >= n_a)
    def _():
        _ffn_tile(xb_ref, gpre_ref, wg_ref, wu_ref, wd_ref, gpost_ref, ob_ref, acc_ref)


def _ffn(xa, xb, g_pre, w_gate, w_up, w_down, g_post, to_cast=()):
    n_a = xa.shape[0] // FFN_ROWS
    n_b = xb.shape[0] // FFN_ROWS
    spec_a = pl.BlockSpec((FFN_ROWS, D_MODEL), lambda i: (jnp.minimum(i, n_a - 1), 0))
    spec_b = pl.BlockSpec((FFN_ROWS, D_MODEL), lambda i: (jnp.maximum(i - n_a, 0), 0))
    n_slabs = CAST_SLABS
    assert n_slabs <= n_a + n_b and all(w.shape[0] % (n_slabs * BF16_SUBLANES) == 0 for w in to_cast)
    cast_specs = [pl.BlockSpec((w.shape[0] // n_slabs, w.shape[1]),
                               lambda i: (jnp.minimum(i, n_slabs - 1), 0)) for w in to_cast]
    return pl.pallas_call(
        functools.partial(_ffn_kernel, n_a, len(to_cast)),
        out_shape=(jax.ShapeDtypeStruct(xa.shape, F32), jax.ShapeDtypeStruct(xb.shape, F32),
                   *[jax.ShapeDtypeStruct(w.shape, BF16) for w in to_cast]),
        grid=(n_a + n_b,),
        in_specs=[
            spec_a, spec_b,
            _const_spec((1, D_MODEL)),
            _const_spec((D_MODEL, FFN_DIM)),
            _const_spec((D_MODEL, FFN_DIM)),
            _const_spec((FFN_DIM, D_MODEL)),
            _const_spec((1, D_MODEL)),
            *cast_specs,
        ],
        out_specs=(spec_a, spec_b, *cast_specs),
        scratch_shapes=[pltpu.VMEM((FFN_ROWS, D_MODEL), F32)],
        compiler_params=pltpu.CompilerParams(
            dimension_semantics=("arbitrary",), vmem_limit_bytes=FFN_VMEM_LIMIT),
        name="ffn",
    )(xa, xb, g_pre, w_gate, w_up, w_down, g_post, *to_cast)


def _mem_kv_kernel(m_ref, g_ref, wk_ref, wv_ref, kt_ref, vt_ref, kb_ref, vb_ref):
    mn = _rms(m_ref[...], g_ref[...]).astype(BF16)
    halves = X_HEAD_DIM // LANES
    for w_ref, t_ref, b_ref in ((wk_ref, kt_ref, kb_ref), (wv_ref, vt_ref, vb_ref)):
        x = _dot(mn, w_ref[...])
        b_ref[...] = x.astype(BF16)
        for hd in range(X_HEADS):
            for j in range(halves):
                c0 = hd * X_HEAD_DIM + j * LANES
                t_ref[pl.ds(j * X_HEADS + hd, MEM_TOKENS, stride=X_HEADS * halves), :] = x[:, c0:c0 + LANES]


def _mem_kv(mem, g, w_k, w_v):
    bsz = mem.shape[0] // MEM_TOKENS
    tiled_rows = MEM_TOKENS * D_MODEL // LANES
    tiled = jax.ShapeDtypeStruct((bsz, tiled_rows, LANES), F32)
    dense = jax.ShapeDtypeStruct((bsz, MEM_TOKENS, D_MODEL), BF16)
    tiled_spec = pl.BlockSpec((None, tiled_rows, LANES), lambda i: (i, 0, 0))
    dense_spec = pl.BlockSpec((None, MEM_TOKENS, D_MODEL), lambda i: (i, 0, 0))
    return pl.pallas_call(
        _mem_kv_kernel,
        out_shape=(tiled, tiled, dense, dense),
        grid=(bsz,),
        in_specs=[
            pl.BlockSpec((MEM_TOKENS, D_MODEL), lambda i: (i, 0)),
            _const_spec((1, D_MODEL)),
            _const_spec((D_MODEL, D_MODEL)),
            _const_spec((D_MODEL, D_MODEL)),
        ],
        out_specs=(tiled_spec, tiled_spec, dense_spec, dense_spec),
        compiler_params=pltpu.CompilerParams(
            dimension_semantics=("arbitrary",), vmem_limit_bytes=VMEM_LIMIT),
        name="mem_kv",
    )(mem, g, w_k, w_v)


def _attend(q, k, v):
    outs = []
    for hd in range(X_HEADS):
        cols = slice(hd * X_HEAD_DIM, (hd + 1) * X_HEAD_DIM)
        s = _dot_nt(q[:, cols].astype(BF16), k[:, cols]) * (X_HEAD_DIM ** -0.5)
        e = jnp.exp(s - jnp.max(s, axis=-1, keepdims=True))
        p = e / jnp.sum(e, axis=-1, keepdims=True)
        outs.append(_dot(p.astype(BF16), v[:, cols]).astype(BF16))
    return jnp.concatenate(outs, axis=-1)


def _conv_block(cbuf_ref, cw_ref, base, r0, c0):
    nrow = CONV_BLOCK_ROWS
    lanes = slice(c0, c0 + LANES)
    y = None
    for res in range(SUBLANES):
        taps = [k for k in range(CONV_K) if (base + k) % SUBLANES == res]
        ext = nrow + (SUBLANES if res else 0)
        part = None
        for k in taps:
            start = r0 + base + k - res
            term = cbuf_ref[start:start + ext, lanes] * cw_ref[k:k + 1, lanes]
            part = term if part is None else part + term
        part = part[res:res + nrow, :]
        y = part if y is None else y + part
    return y


def _mix_prompt_kernel(tiles_per_seq, h_ref, hprev_ref, k_ref, v_ref, gpre_ref, win_ref, slng_ref, slnb_ref,
                       ws_ref, bs_ref, cw_ref, cb_ref, clng_ref, clnb_ref, wout_ref, gpost_ref,
                       xgpre_ref, wq_ref, wo_ref, xgpost_ref,
                       o_ref, cstate_ref, chunkv_ref, cbuf_ref, mixed_ref, res_ref):
    s = pl.program_id(0)
    rows = MIX_ROWS
    tile = jnp.minimum(s, pl.num_programs(0) - 2)
    last_of_seq = (tile % tiles_per_seq) == tiles_per_seq - 1

    @pl.when(s == 0)
    def _():
        cbuf_ref[0:CARRY_ROWS, :] = jnp.zeros((CARRY_ROWS, CONV_WIDTH), F32)
        mixed_ref[...] = jnp.zeros((rows, D_MODEL), BF16)

    n = _rms(h_ref[...], gpre_ref[...]).astype(BF16)
    z_conv = _dot(n, win_ref[:, 2 * SGU_WIDTH:])
    a = z_conv[:, 0:CONV_WIDTH]
    gate = z_conv[:, CONV_WIDTH:]
    cbuf_ref[CARRY_ROWS:CARRY_ROWS + rows, :] = a * jax.nn.sigmoid(gate)
    z_sgu = _dot(n, win_ref[:, 0:2 * SGU_WIDTH])
    u = z_sgu[:, 0:SGU_WIDTH]
    v = _ln(z_sgu[:, SGU_WIDTH:], slng_ref[...], slnb_ref[...])
    chunkv_ref[...] = v[rows - CHUNK:, :]

    res_ref[...] = hprev_ref[...] + _rms(_dot(mixed_ref[...], wout_ref[...]), gpost_ref[...])
    q = _dot(_rms(res_ref[...], xgpre_ref[...]).astype(BF16), wq_ref[...])

    row_id = lax.broadcasted_iota(jnp.int32, (CHUNK, CHUNK), 0)
    col_id = lax.broadcasted_iota(jnp.int32, (CHUNK, CHUNK), 1)
    vb = v.astype(BF16)
    s_heads = []
    for hd in range(SGU_HEADS):
        ws = jnp.where(row_id >= col_id, ws_ref[hd], 0.0).astype(BF16)
        cols = slice(hd * SGU_HEAD_DIM, (hd + 1) * SGU_HEAD_DIM)
        s_chunks = [_dot(ws, vb[c * CHUNK:(c + 1) * CHUNK, cols]) for c in range(rows // CHUNK)]
        s_heads.append(jnp.concatenate(s_chunks, axis=0))
    bias = jnp.concatenate([bs_ref[...]] * (rows // CHUNK), axis=0)
    out_a = u * (jnp.concatenate(s_heads, axis=-1) + bias)

    att = _attend(q, k_ref[...], v_ref[...])
    o_ref[...] = res_ref[...] + _rms(_dot(att, wo_ref[...]), xgpost_ref[...])

    base = CARRY_ROWS - (CONV_K - 1)
    conv = jnp.concatenate(
        [jnp.concatenate([_conv_block(cbuf_ref, cw_ref, base, r0, c0)
                          for c0 in range(0, CONV_WIDTH, LANES)], axis=-1)
         for r0 in range(0, rows, CONV_BLOCK_ROWS)], axis=0) + cb_ref[...]
    out_b = _silu(_ln(conv, clng_ref[...], clnb_ref[...]))
    cstate_ref[...] = cbuf_ref[rows + base:rows + CARRY_ROWS, :]
    cbuf_ref[0:CARRY_ROWS, :] = jnp.where(last_of_seq, 0.0, cbuf_ref[rows:rows + CARRY_ROWS, :])
    mixed_ref[...] = jnp.concatenate([out_a.astype(BF16), out_b.astype(BF16)], axis=-1)


def _mix_prompt(h, k, v, p, tiles_per_seq):
    n_tiles = h.shape[0] // MIX_ROWS
    bsz = n_tiles // tiles_per_seq
    mix_tile = lambda s: jnp.minimum(s, n_tiles - 1)
    att_tile = lambda s: jnp.maximum(s - 1, 0)
    mem = pl.BlockSpec((None, MEM_TOKENS, D_MODEL), lambda s: (att_tile(s) // tiles_per_seq, 0, 0))
    return pl.pallas_call(
        functools.partial(_mix_prompt_kernel, tiles_per_seq),
        out_shape=(jax.ShapeDtypeStruct(h.shape, F32),
                   jax.ShapeDtypeStruct((bsz, CONV_K - 1, CONV_WIDTH), F32),
                   jax.ShapeDtypeStruct((bsz, CHUNK, SGU_WIDTH), F32)),
        grid=(n_tiles + 1,),
        in_specs=[
            pl.BlockSpec((MIX_ROWS, D_MODEL), lambda s: (mix_tile(s), 0)),
            pl.BlockSpec((MIX_ROWS, D_MODEL), lambda s: (att_tile(s), 0)), mem, mem,
            _const_spec((1, D_MODEL)),
            _const_spec((D_MODEL, 2 * SGU_WIDTH + 2 * CONV_WIDTH)),
            _const_spec((1, SGU_WIDTH)), _const_spec((1, SGU_WIDTH)),
            _const_spec((SGU_HEADS, CHUNK, CHUNK)),
            _const_spec((CHUNK, SGU_WIDTH)),
            _const_spec((CONV_K, CONV_WIDTH)), _const_spec((1, CONV_WIDTH)),
            _const_spec((1, CONV_WIDTH)), _const_spec((1, CONV_WIDTH)),
            _const_spec((D_MODEL, D_MODEL)), _const_spec((1, D_MODEL)),
            _const_spec((1, D_MODEL)),
            _const_spec((D_MODEL, D_MODEL)), _const_spec((D_MODEL, D_MODEL)),
            _const_spec((1, D_MODEL)),
        ],
        out_specs=(pl.BlockSpec((MIX_ROWS, D_MODEL), lambda s: (att_tile(s), 0)),
                   pl.BlockSpec((None, CONV_K - 1, CONV_WIDTH),
                                lambda s: (mix_tile(s) // tiles_per_seq, 0, 0)),
                   pl.BlockSpec((None, CHUNK, SGU_WIDTH),
                                lambda s: (mix_tile(s) // tiles_per_seq, 0, 0))),
        scratch_shapes=[pltpu.VMEM((MIX_ROWS + CARRY_ROWS, CONV_WIDTH), F32),
                        pltpu.VMEM((MIX_ROWS, D_MODEL), BF16),
                        pltpu.VMEM((MIX_ROWS, D_MODEL), F32)],
        compiler_params=pltpu.CompilerParams(
            dimension_semantics=("arbitrary",), vmem_limit_bytes=VMEM_LIMIT),
        name="mix_prompt",
    )(h, h, k, v, p["g_pre"], p["w_in"], p["sln_g"], p["sln_b"], p["w_s"], p["b_s_full"],
      p["conv_w"], p["conv_b"], p["cln_g"], p["cln_b"], p["w_out"], p["g_post"],
      p["xg_pre"], p["w_q"], p["w_o"], p["xg_post"])


def _cached_head(c_ref, b, hd):
    halves = X_HEAD_DIM // LANES
    stride = X_HEADS * halves
    parts = [c_ref[b, pl.ds(j * X_HEADS + hd, MEM_TOKENS, stride=stride), :] for j in range(halves)]
    return jnp.concatenate(parts, axis=-1).astype(BF16)


def _mix_sample_kernel(dec_seq, h_ref, st_ref, k_ref, v_ref, gpre_ref, win_ref, slng_ref, slnb_ref,
                       wsx_ref, bsx_ref, cwx_ref, cb_ref, clng_ref, clnb_ref, wout_ref, gpost_ref,
                       xgpre_ref, wq_ref, wo_ref, xgpost_ref,
                       o_ref, cstate_ref, chunkv_ref, glu_ref):
    bt = SAMPLE_BATCH_TILE
    hist = CONV_K - 1
    h = h_ref[...]
    n = _rms(h, gpre_ref[...]).astype(BF16)
    z = _dot(n, win_ref[...])
    u = z[:, 0:SGU_WIDTH]
    v = _ln(z[:, SGU_WIDTH:2 * SGU_WIDTH], slng_ref[...], slnb_ref[...])
    a = z[:, 2 * SGU_WIDTH:2 * SGU_WIDTH + CONV_WIDTH]
    gate = z[:, 2 * SGU_WIDTH + CONV_WIDTH:]
    chunkv_ref[...] = v

    v3 = v.reshape(bt, dec_seq, SGU_WIDTH)
    s = jnp.broadcast_to(bsx_ref[...][None], (bt, dec_seq, SGU_WIDTH))
    for j in range(dec_seq):
        s = s + wsx_ref[j][None] * v3[:, j:j + 1, :]
    out_a = u * s.reshape(bt * dec_seq, SGU_WIDTH)

    glu = a * jax.nn.sigmoid(gate)
    glu3 = glu.reshape(bt, dec_seq, CONV_WIDTH)
    new_part = jnp.broadcast_to(cb_ref[...][None], (bt, dec_seq, CONV_WIDTH))
    for r in range(dec_seq):
        new_part = new_part + cwx_ref[hist + r][None] * glu3[:, r:r + 1, :]
    old_parts = []
    for b in range(bt):
        acc = cwx_ref[0] * st_ref[0, b:b + 1, :]
        for r in range(1, hist):
            acc = acc + cwx_ref[r] * st_ref[r, b:b + 1, :]
        old_parts.append(acc)
    conv = new_part.reshape(bt * dec_seq, CONV_WIDTH) + jnp.concatenate(old_parts, axis=0)
    out_b = _silu(_ln(conv, clng_ref[...], clnb_ref[...]))

    cstate_ref[0:hist - dec_seq, :, :] = st_ref[dec_seq:hist, :, :]
    for sl in range(CONV_WIDTH // LANES):
        glu_ref[sl] = glu[:, sl * LANES:(sl + 1) * LANES]
    for tt in range(dec_seq):
        for sl in range(CONV_WIDTH // LANES):
            cstate_ref[hist - dec_seq + tt, :, sl * LANES:(sl + 1) * LANES] = (
                glu_ref[sl, pl.ds(tt, bt, stride=dec_seq), :])

    mixed = jnp.concatenate([out_a.astype(BF16), out_b.astype(BF16)], axis=-1)
    h2 = h + _rms(_dot(mixed, wout_ref[...]), gpost_ref[...])

    q = _dot(_rms(h2, xgpre_ref[...]).astype(BF16), wq_ref[...])
    scores = []
    for b in range(bt):
        for hd in range(X_HEADS):
            qbh = q[b * dec_seq:(b + 1) * dec_seq, hd * X_HEAD_DIM:(hd + 1) * X_HEAD_DIM]
            scores.append(_dot_nt(qbh.astype(BF16), _cached_head(k_ref, b, hd)))
    sc = jnp.concatenate(scores, axis=0) * (X_HEAD_DIM ** -0.5)
    e = jnp.exp(sc - jnp.max(sc, axis=-1, keepdims=True))
    pr = e / jnp.sum(e, axis=-1, keepdims=True)
    atts = []
    for b in range(bt):
        heads = []
        for hd in range(X_HEADS):
            i0 = (b * X_HEADS + hd) * dec_seq
            heads.append(_dot(pr[i0:i0 + dec_seq, :].astype(BF16), _cached_head(v_ref, b, hd)))
        atts.append(jnp.concatenate(heads, axis=-1))
    att = jnp.concatenate(atts, axis=0).astype(BF16)
    o_ref[...] = h2 + _rms(_dot(att, wo_ref[...]), xgpost_ref[...])


def _mix_sample(h, state, k, v, p, dec_seq):
    n_rows = h.shape[0]
    bsz = n_rows // dec_seq
    bt = SAMPLE_BATCH_TILE
    hist = CONV_K - 1
    tok = pl.BlockSpec((bt * dec_seq, D_MODEL), lambda i: (i, 0))
    mem = pl.BlockSpec((bt,) + k.shape[1:], lambda i: (i, 0, 0))
    st = pl.BlockSpec((hist, bt, CONV_WIDTH), lambda i: (0, i, 0))
    return pl.pallas_call(
        functools.partial(_mix_sample_kernel, dec_seq),
        out_shape=(jax.ShapeDtypeStruct((n_rows, D_MODEL), F32),
                   jax.ShapeDtypeStruct((hist, bsz, CONV_WIDTH), F32),
                   jax.ShapeDtypeStruct((n_rows, SGU_WIDTH), F32)),
        grid=(bsz // bt,),
        in_specs=[
            tok, st, mem, mem,
            _const_spec((1, D_MODEL)),
            _const_spec((D_MODEL, 2 * SGU_WIDTH + 2 * CONV_WIDTH)),
            _const_spec((1, SGU_WIDTH)), _const_spec((1, SGU_WIDTH)),
            _const_spec((dec_seq, dec_seq, SGU_WIDTH)),
            _const_spec((dec_seq, SGU_WIDTH)),
            _const_spec((hist + dec_seq, dec_seq, CONV_WIDTH)), _const_spec((1, CONV_WIDTH)),
            _const_spec((1, CONV_WIDTH)), _const_spec((1, CONV_WIDTH)),
            _const_spec((D_MODEL, D_MODEL)), _const_spec((1, D_MODEL)),
            _const_spec((1, D_MODEL)),
            _const_spec((D_MODEL, D_MODEL)), _const_spec((D_MODEL, D_MODEL)),
            _const_spec((1, D_MODEL)),
        ],
        out_specs=(tok, st, pl.BlockSpec((bt * dec_seq, SGU_WIDTH), lambda i: (i, 0))),
        scratch_shapes=[pltpu.VMEM((CONV_WIDTH // LANES, bt * dec_seq, LANES), F32)],
        compiler_params=pltpu.CompilerParams(
            dimension_semantics=("arbitrary",), vmem_limit_bytes=VMEM_LIMIT),
        name="mix_sample",
    )(h, state, k, v, p["g_pre"], p["w_in"], p["sln_g"], p["sln_b"], p["w_s_rows"], p["b_s_rows"],
      p["conv_w_rows"], p["conv_b"], p["cln_g"], p["cln_b"], p["w_out"], p["g_post"],
      p["xg_pre"], p["w_q"], p["w_o"], p["xg_post"])


def _head_tiled_rows(c):
    b, m, hh, dh = c.shape
    c = c.reshape(b, m, hh, dh // LANES, LANES)
    return jnp.transpose(c, (0, 1, 3, 2, 4)).reshape(b, m * hh * (dh // LANES), LANES)


def _from_head_tiled_rows(t):
    b = t.shape[0]
    halves = X_HEAD_DIM // LANES
    t = t.reshape(b, MEM_TOKENS, halves, X_HEADS, LANES)
    return jnp.transpose(t, (0, 1, 3, 2, 4)).reshape(b, MEM_TOKENS, X_HEADS, X_HEAD_DIM)


def _row(x):
    return x.reshape(1, -1)


def kernel(x_prompt, x_sample, mem_prompt, cache_mem_k, cache_mem_v, state_conv, ffn1_g_pre, ffn1_w_gate, ffn1_w_up, ffn1_w_down, ffn1_g_post, mix_g_pre, mix_w_in, sgu_ln_g, sgu_ln_b, sgu_w_s, sgu_b_s, conv_w, conv_b, conv_ln_g, conv_ln_b, mix_w_out, mix_g_post, mem_g, xattn_g_pre, xattn_w_q, xattn_w_k, xattn_w_v, xattn_w_o, xattn_g_post, ffn2_g_pre, ffn2_w_gate, ffn2_w_up, ffn2_w_down, ffn2_g_post):
    depth = ffn1_g_pre.shape[0]
    bsz, seq, _ = x_prompt.shape
    dbsz, dec_seq, _ = x_sample.shape
    hist = CONV_K - 1
    assert seq % MIX_ROWS == 0 and dec_seq <= CHUNK and dec_seq <= hist

    hp = x_prompt.reshape(bsz * seq, D_MODEL)
    hs = x_sample.reshape(dbsz * dec_seq, D_MODEL)
    mem = mem_prompt.reshape(bsz * MEM_TOKENS, D_MODEL)
    outs = [[] for _ in range(6)]
    for l in range(depth):
        ffn1 = (_row(ffn1_g_pre[l]), ffn1_w_gate[l].astype(BF16), ffn1_w_up[l].astype(BF16),
                ffn1_w_down[l].astype(BF16), _row(ffn1_g_post[l]))
        w_s, b_s, cw = sgu_w_s[l], sgu_b_s[l], conv_w[l]
        tril = jnp.tril(w_s[:, :dec_seq, :dec_seq])
        w_s_rows = jnp.repeat(jnp.transpose(tril, (2, 1, 0)), SGU_HEAD_DIM, axis=2)
        b_s_rows = jnp.repeat(b_s[:, :dec_seq].T, SGU_HEAD_DIM, axis=1)
        tap = jnp.arange(hist + dec_seq)[:, None] - jnp.arange(dec_seq)[None, :]
        conv_w_rows = jnp.where(((tap >= 0) & (tap < CONV_K))[:, :, None],
                                cw[jnp.clip(tap, 0, CONV_K - 1)], 0.0)
        p = dict(
            g_pre=_row(mix_g_pre[l]),
            sln_g=_row(sgu_ln_g[l]), sln_b=_row(sgu_ln_b[l]),
            w_s=w_s, b_s_full=jnp.repeat(b_s.T, SGU_HEAD_DIM, axis=1),
            w_s_rows=w_s_rows, b_s_rows=b_s_rows,
            conv_w=cw, conv_w_rows=conv_w_rows, conv_b=_row(conv_b[l]),
            cln_g=_row(conv_ln_g[l]), cln_b=_row(conv_ln_b[l]),
            g_post=_row(mix_g_post[l]), xg_pre=_row(xattn_g_pre[l]), xg_post=_row(xattn_g_post[l]),
        )

        k_t, v_t, k_b, v_b = _mem_kv(mem, _row(mem_g[l]), xattn_w_k[l].astype(BF16),
                                     xattn_w_v[l].astype(BF16))

        later = (ffn2_w_gate[l], ffn2_w_up[l], ffn2_w_down[l],
                 mix_w_in[l], mix_w_out[l], xattn_w_q[l], xattn_w_o[l])
        hp, hs, w2_gate, w2_up, w2_down, p["w_in"], p["w_out"], p["w_q"], p["w_o"] = _ffn(
            hp, hs, *ffn1, to_cast=later)
        hp, cs_p, cv_p = _mix_prompt(hp, k_b, v_b, p, seq // MIX_ROWS)
        hs, cs_s, cv_s = _mix_sample(hs, jnp.transpose(state_conv[l], (1, 0, 2)),
                                     _head_tiled_rows(cache_mem_k[l]),
                                     _head_tiled_rows(cache_mem_v[l]), p, dec_seq)
        hp, hs = _ffn(hp, hs, _row(ffn2_g_pre[l]), w2_gate, w2_up, w2_down, _row(ffn2_g_post[l]))

        outs[0].append(_from_head_tiled_rows(k_t))
        outs[1].append(_from_head_tiled_rows(v_t))
        outs[2].append(cs_p)
        outs[3].append(jnp.transpose(cs_s, (1, 0, 2)))
        outs[4].append(cv_p)
        outs[5].append(cv_s.reshape(dbsz, dec_seq, SGU_WIDTH))

    return (hp.reshape(bsz, seq, D_MODEL), hs.reshape(dbsz, dec_seq, D_MODEL),
            jnp.stack(outs[0]), jnp.stack(outs[1]), jnp.stack(outs[2]), jnp.stack(outs[3]),
            jnp.stack(outs[4]), jnp.stack(outs[5]))
```

```python
import functools

import jax
import jax.numpy as jnp
from jax import lax
from jax.experimental import pallas as pl
from jax.experimental.pallas import tpu as pltpu

D_MODEL = 1024
SGU_WIDTH = 512
CONV_WIDTH = 512
SGU_HEADS = 4
SGU_HEAD_DIM = 128
CHUNK = 128
CONV_K = 31
MEM_TOKENS = 256
X_HEADS = 4
X_HEAD_DIM = 256
FFN_DIM = 4096
EPS = 1e-6
LANES = 128
SUBLANES = 8

F32 = jnp.float32
BF16 = jnp.bfloat16

FFN_ROWS = 512
FFN_COLS = 1024
MIX_ROWS = 512
SAMPLE_BATCH_TILE = 8
CARRY_ROWS = 32
CONV_BLOCK_ROWS = 128
VMEM_LIMIT = 56 * 1024 * 1024
FFN_VMEM_LIMIT = 60 * 1024 * 1024
CAST_SLABS = 32
BF16_SUBLANES = 16
STAGE_ROWS = 256
STAGE_COLS = 1024
STAGE_SLOTS = 4


def _rms(x, g):
    return x * lax.rsqrt(jnp.mean(x * x, axis=-1, keepdims=True) + EPS) * g


def _ln(x, g, b):
    mu = jnp.mean(x, axis=-1, keepdims=True)
    xc = x - mu
    return xc * lax.rsqrt(jnp.mean(xc * xc, axis=-1, keepdims=True) + EPS) * g + b


def _sigmoid(x):
    return 0.5 + 0.5 * jnp.tanh(0.5 * x)


def _silu(x):
    half = 0.5 * x
    return half + half * jnp.tanh(half)


def _dot(a, b):
    return jnp.dot(a, b, preferred_element_type=F32)


def _dot_nt(a, b):
    return lax.dot_general(a, b, (((1,), (1,)), ((), ())), preferred_element_type=F32)


def _const_spec(shape):
    nd = len(shape)
    return pl.BlockSpec(shape, lambda *_: (0,) * nd, pipeline_mode=pl.Buffered(1))


def _ffn_tile(x_ref, gpre_ref, wg_ref, wu_ref, wd_ref, gpost_ref, o_ref, acc_ref):
    x = x_ref[...]
    n = _rms(x, gpre_ref[...]).astype(BF16)
    for c in range(FFN_DIM // FFN_COLS):
        cols = slice(c * FFN_COLS, (c + 1) * FFN_COLS)
        g = _dot(n, wg_ref[:, cols])
        u = _dot(n, wu_ref[:, cols])
        h = (_silu(g) * u).astype(BF16)
        d = _dot(h, wd_ref[cols, :])
        if c == 0:
            acc_ref[...] = d
        else:
            acc_ref[...] += d
    o_ref[...] = x + 0.5 * _rms(acc_ref[...], gpost_ref[...])


def _stage_and_convert(w_hbm, w_ref, stage_ref, sems):
    pieces = [(r0, c0) for r0 in range(0, w_hbm.shape[0], STAGE_ROWS)
              for c0 in range(0, w_hbm.shape[1], STAGE_COLS)]

    def copy(idx):
        r0, c0 = pieces[idx]
        slot = idx % STAGE_SLOTS
        return pltpu.make_async_copy(w_hbm.at[pl.ds(r0, STAGE_ROWS), pl.ds(c0, STAGE_COLS)],
                                     stage_ref.at[slot], sems.at[slot])

    for idx in range(min(STAGE_SLOTS, len(pieces))):
        copy(idx).start()
    for idx, (r0, c0) in enumerate(pieces):
        copy(idx).wait()
        w_ref[r0:r0 + STAGE_ROWS, c0:c0 + STAGE_COLS] = stage_ref[idx % STAGE_SLOTS].astype(BF16)
        if idx + STAGE_SLOTS < len(pieces):
            copy(idx + STAGE_SLOTS).start()


def _ffn_kernel(n_a, n_cast, own_f32, xa_ref, xb_ref, gpre_ref, wg_ref, wu_ref, wd_ref, gpost_ref, *refs):
    cast_src = refs[:n_cast]
    oa_ref, ob_ref = refs[n_cast:n_cast + 2]
    cast_dst = refs[n_cast + 2:2 * n_cast + 2]
    acc_ref = refs[2 * n_cast + 2]
    i = pl.program_id(0)

    if own_f32:
        wg_hbm, wu_hbm, wd_hbm = wg_ref, wu_ref, wd_ref
        wg_ref, wu_ref, wd_ref, stage_ref, sems = refs[2 * n_cast + 3:]

        @pl.when(i == 0)
        def _():
            for w_hbm, w_ref in ((wg_hbm, wg_ref), (wu_hbm, wu_ref), (wd_hbm, wd_ref)):
                _stage_and_convert(w_hbm, w_ref, stage_ref, sems)

    for src, dst in zip(cast_src, cast_dst):
        dst[...] = src[...].astype(BF16)

    @pl.when(i < n_a)
    def _():
        _ffn_tile(xa_ref, gpre_ref, wg_ref, wu_ref, wd_ref, gpost_ref, oa_ref, acc_ref)

    @pl.when(i >= n_a)
    def _():
        _ffn_tile(xb_ref, gpre_ref, wg_ref, wu_ref, wd_ref, gpost_ref, ob_ref, acc_ref)


def _ffn(xa, xb, g_pre, w_gate, w_up, w_down, g_post, to_cast=()):
    n_a = xa.shape[0] // FFN_ROWS
    n_b = xb.shape[0] // FFN_ROWS
    own_f32 = w_gate.dtype == F32
    assert w_up.dtype == w_gate.dtype and w_down.dtype == w_gate.dtype
    spec_a = pl.BlockSpec((FFN_ROWS, D_MODEL), lambda i: (jnp.minimum(i, n_a - 1), 0))
    spec_b = pl.BlockSpec((FFN_ROWS, D_MODEL), lambda i: (jnp.maximum(i - n_a, 0), 0))
    n_slabs = CAST_SLABS
    assert n_slabs <= n_a + n_b and all(w.shape[0] % (n_slabs * BF16_SUBLANES) == 0 for w in to_cast)
    cast_specs = [pl.BlockSpec((w.shape[0] // n_slabs, w.shape[1]),
                               lambda i: (jnp.minimum(i, n_slabs - 1), 0)) for w in to_cast]
    scratch = [pltpu.VMEM((FFN_ROWS, D_MODEL), F32)]
    if own_f32:
        weight_specs = [pl.BlockSpec(memory_space=pl.ANY)] * 3
        scratch += [pltpu.VMEM((D_MODEL, FFN_DIM), BF16), pltpu.VMEM((D_MODEL, FFN_DIM), BF16),
                    pltpu.VMEM((FFN_DIM, D_MODEL), BF16),
                    pltpu.VMEM((STAGE_SLOTS, STAGE_ROWS, STAGE_COLS), F32),
                    pltpu.SemaphoreType.DMA((STAGE_SLOTS,))]
    else:
        weight_specs = [_const_spec((D_MODEL, FFN_DIM)), _const_spec((D_MODEL, FFN_DIM)),
                        _const_spec((FFN_DIM, D_MODEL))]
    return pl.pallas_call(
        functools.partial(_ffn_kernel, n_a, len(to_cast), own_f32),
        out_shape=(jax.ShapeDtypeStruct(xa.shape, F32), jax.ShapeDtypeStruct(xb.shape, F32),
                   *[jax.ShapeDtypeStruct(w.shape, BF16) for w in to_cast]),
        grid=(n_a + n_b,),
        in_specs=[
            spec_a, spec_b,
            _const_spec((1, D_MODEL)),
            *weight_specs,
            _const_spec((1, D_MODEL)),
            *cast_specs,
        ],
        out_specs=(spec_a, spec_b, *cast_specs),
        scratch_shapes=scratch,
        compiler_params=pltpu.CompilerParams(
            dimension_semantics=("arbitrary",), vmem_limit_bytes=FFN_VMEM_LIMIT),
        name="ffn",
    )(xa, xb, g_pre, w_gate, w_up, w_down, g_post, *to_cast)


def _mem_kv_kernel(m_ref, g_ref, wk_ref, wv_ref, kt_ref, vt_ref, kb_ref, vb_ref):
    mn = _rms(m_ref[...], g_ref[...]).astype(BF16)
    halves = X_HEAD_DIM // LANES
    for w_ref, t_ref, b_ref in ((wk_ref, kt_ref, kb_ref), (wv_ref, vt_ref, vb_ref)):
        x = _dot(mn, w_ref[...])
        b_ref[...] = x.astype(BF16)
        for hd in range(X_HEADS):
            for j in range(halves):
                c0 = hd * X_HEAD_DIM + j * LANES
                t_ref[pl.ds(j * X_HEADS + hd, MEM_TOKENS, stride=X_HEADS * halves), :] = x[:, c0:c0 + LANES]


def _mem_kv(mem, g, w_k, w_v):
    bsz = mem.shape[0] // MEM_TOKENS
    tiled_rows = MEM_TOKENS * D_MODEL // LANES
    tiled = jax.ShapeDtypeStruct((bsz, tiled_rows, LANES), F32)
    dense = jax.ShapeDtypeStruct((bsz, MEM_TOKENS, D_MODEL), BF16)
    tiled_spec = pl.BlockSpec((None, tiled_rows, LANES), lambda i: (i, 0, 0))
    dense_spec = pl.BlockSpec((None, MEM_TOKENS, D_MODEL), lambda i: (i, 0, 0))
    return pl.pallas_call(
        _mem_kv_kernel,
        out_shape=(tiled, tiled, dense, dense),
        grid=(bsz,),
        in_specs=[
            pl.BlockSpec((MEM_TOKENS, D_MODEL), lambda i: (i, 0)),
            _const_spec((1, D_MODEL)),
            _const_spec((D_MODEL, D_MODEL)),
            _const_spec((D_MODEL, D_MODEL)),
        ],
        out_specs=(tiled_spec, tiled_spec, dense_spec, dense_spec),
        compiler_params=pltpu.CompilerParams(
            dimension_semantics=("arbitrary",), vmem_limit_bytes=VMEM_LIMIT),
        name="mem_kv",
    )(mem, g, w_k, w_v)


def _attend(q, k, v):
    outs = []
    for hd in range(X_HEADS):
        cols = slice(hd * X_HEAD_DIM, (hd + 1) * X_HEAD_DIM)
        s = _dot_nt(q[:, cols].astype(BF16), k[:, cols]) * (X_HEAD_DIM ** -0.5)
        e = jnp.exp(s - jnp.max(s, axis=-1, keepdims=True))
        p = e / jnp.sum(e, axis=-1, keepdims=True)
        outs.append(_dot(p.astype(BF16), v[:, cols]).astype(BF16))
    return jnp.concatenate(outs, axis=-1)


def _conv_block(cbuf_ref, cw_ref, base, r0, c0):
    nrow = CONV_BLOCK_ROWS
    lanes = slice(c0, c0 + LANES)
    y = None
    for res in range(SUBLANES):
        taps = [k for k in range(CONV_K) if (base + k) % SUBLANES == res]
        ext = nrow + (SUBLANES if res else 0)
        part = None
        for k in taps:
            start = r0 + base + k - res
            term = cbuf_ref[start:start + ext, lanes] * cw_ref[k:k + 1, lanes]
            part = term if part is None else part + term
        part = part[res:res + nrow, :]
        y = part if y is None else y + part
    return y


def _mix_prompt_kernel(tiles_per_seq, h_ref, hprev_ref, k_ref, v_ref, gpre_ref, win_ref, slng_ref, slnb_ref,
                       ws_ref, bs_ref, cw_ref, cb_ref, clng_ref, clnb_ref, wout_ref, gpost_ref,
                       xgpre_ref, wq_ref, wo_ref, xgpost_ref,
                       o_ref, cstate_ref, chunkv_ref, cbuf_ref, mixed_ref, res_ref):
    s = pl.program_id(0)
    rows = MIX_ROWS
    tile = jnp.minimum(s, pl.num_programs(0) - 2)
    last_of_seq = (tile % tiles_per_seq) == tiles_per_seq - 1

    @pl.when(s == 0)
    def _():
        cbuf_ref[0:CARRY_ROWS, :] = jnp.zeros((CARRY_ROWS, CONV_WIDTH), F32)
        mixed_ref[...] = jnp.zeros((rows, D_MODEL), BF16)

    n = _rms(h_ref[...], gpre_ref[...]).astype(BF16)
    z_conv = _dot(n, win_ref[:, 2 * SGU_WIDTH:])
    a = z_conv[:, 0:CONV_WIDTH]
    gate = z_conv[:, CONV_WIDTH:]
    cbuf_ref[CARRY_ROWS:CARRY_ROWS + rows, :] = a * _sigmoid(gate)
    z_sgu = _dot(n, win_ref[:, 0:2 * SGU_WIDTH])
    u = z_sgu[:, 0:SGU_WIDTH]
    v = _ln(z_sgu[:, SGU_WIDTH:], slng_ref[...], slnb_ref[...])
    chunkv_ref[...] = v[rows - CHUNK:, :]

    res_ref[...] = hprev_ref[...] + _rms(_dot(mixed_ref[...], wout_ref[...]), gpost_ref[...])
    q = _dot(_rms(res_ref[...], xgpre_ref[...]).astype(BF16), wq_ref[...])

    row_id = lax.broadcasted_iota(jnp.int32, (CHUNK, CHUNK), 0)
    col_id = lax.broadcasted_iota(jnp.int32, (CHUNK, CHUNK), 1)
    vb = v.astype(BF16)
    s_heads = []
    for hd in range(SGU_HEADS):
        ws = jnp.where(row_id >= col_id, ws_ref[hd], 0.0).astype(BF16)
        cols = slice(hd * SGU_HEAD_DIM, (hd + 1) * SGU_HEAD_DIM)
        s_chunks = [_dot(ws, vb[c * CHUNK:(c + 1) * CHUNK, cols]) for c in range(rows // CHUNK)]
        s_heads.append(jnp.concatenate(s_chunks, axis=0))
    bias = jnp.concatenate([bs_ref[...]] * (rows // CHUNK), axis=0)
    out_a = u * (jnp.concatenate(s_heads, axis=-1) + bias)

    att = _attend(q, k_ref[...], v_ref[...])
    o_ref[...] = res_ref[...] + _rms(_dot(att, wo_ref[...]), xgpost_ref[...])

    base = CARRY_ROWS - (CONV_K - 1)
    conv = jnp.concatenate(
        [jnp.concatenate([_conv_block(cbuf_ref, cw_ref, base, r0, c0)
                          for c0 in range(0, CONV_WIDTH, LANES)], axis=-1)
         for r0 in range(0, rows, CONV_BLOCK_ROWS)], axis=0) + cb_ref[...]
    out_b = _silu(_ln(conv, clng_ref[...], clnb_ref[...]))
    cstate_ref[...] = cbuf_ref[rows + base:rows + CARRY_ROWS, :]
    cbuf_ref[0:CARRY_ROWS, :] = jnp.where(last_of_seq, 0.0, cbuf_ref[rows:rows + CARRY_ROWS, :])
    mixed_ref[...] = jnp.concatenate([out_a.astype(BF16), out_b.astype(BF16)], axis=-1)


def _mix_prompt(h, k, v, p, tiles_per_seq):
    n_tiles = h.shape[0] // MIX_ROWS
    bsz = n_tiles // tiles_per_seq
    mix_tile = lambda s: jnp.minimum(s, n_tiles - 1)
    att_tile = lambda s: jnp.maximum(s - 1, 0)
    mem = pl.BlockSpec((None, MEM_TOKENS, D_MODEL), lambda s: (att_tile(s) // tiles_per_seq, 0, 0))
    return pl.pallas_call(
        functools.partial(_mix_prompt_kernel, tiles_per_seq),
        out_shape=(jax.ShapeDtypeStruct(h.shape, F32),
                   jax.ShapeDtypeStruct((bsz, CONV_K - 1, CONV_WIDTH), F32),
                   jax.ShapeDtypeStruct((bsz, CHUNK, SGU_WIDTH), F32)),
        grid=(n_tiles + 1,),
        in_specs=[
            pl.BlockSpec((MIX_ROWS, D_MODEL), lambda s: (mix_tile(s), 0)),
            pl.BlockSpec((MIX_ROWS, D_MODEL), lambda s: (att_tile(s), 0)), mem, mem,
            _const_spec((1, D_MODEL)),
            _const_spec((D_MODEL, 2 * SGU_WIDTH + 2 * CONV_WIDTH)),
            _const_spec((1, SGU_WIDTH)), _const_spec((1, SGU_WIDTH)),
            _const_spec((SGU_HEADS, CHUNK, CHUNK)),
            _const_spec((CHUNK, SGU_WIDTH)),
            _const_spec((CONV_K, CONV_WIDTH)), _const_spec((1, CONV_WIDTH)),
            _const_spec((1, CONV_WIDTH)), _const_spec((1, CONV_WIDTH)),
            _const_spec((D_MODEL, D_MODEL)), _const_spec((1, D_MODEL)),
            _const_spec((1, D_MODEL)),
            _const_spec((D_MODEL, D_MODEL)), _const_spec((D_MODEL, D_MODEL)),
            _const_spec((1, D_MODEL)),
        ],
        out_specs=(pl.BlockSpec((MIX_ROWS, D_MODEL), lambda s: (att_tile(s), 0)),
                   pl.BlockSpec((None, CONV_K - 1, CONV_WIDTH),
                                lambda s: (mix_tile(s) // tiles_per_seq, 0, 0)),
                   pl.BlockSpec((None, CHUNK, SGU_WIDTH),
                                lambda s: (mix_tile(s) // tiles_per_seq, 0, 0))),
        scratch_shapes=[pltpu.VMEM((MIX_ROWS + CARRY_ROWS, CONV_WIDTH), F32),
                        pltpu.VMEM((MIX_ROWS, D_MODEL), BF16),
                        pltpu.VMEM((MIX_ROWS, D_MODEL), F32)],
        compiler_params=pltpu.CompilerParams(
            dimension_semantics=("arbitrary",), vmem_limit_bytes=VMEM_LIMIT),
        name="mix_prompt",
    )(h, h, k, v, p["g_pre"], p["w_in"], p["sln_g"], p["sln_b"], p["w_s"], p["b_s_full"],
      p["conv_w"], p["conv_b"], p["cln_g"], p["cln_b"], p["w_out"], p["g_post"],
      p["xg_pre"], p["w_q"], p["w_o"], p["xg_post"])


def _cached_head(c_ref, b, hd):
    halves = X_HEAD_DIM // LANES
    stride = X_HEADS * halves
    parts = [c_ref[b, pl.ds(j * X_HEADS + hd, MEM_TOKENS, stride=stride), :] for j in range(halves)]
    return jnp.concatenate(parts, axis=-1).astype(BF16)


def _mix_sample_kernel(dec_seq, h_ref, st_ref, k_ref, v_ref, gpre_ref, win_ref, slng_ref, slnb_ref,
                       wsx_ref, bsx_ref, cwx_ref, cb_ref, clng_ref, clnb_ref, wout_ref, gpost_ref,
                       xgpre_ref, wq_ref, wo_ref, xgpost_ref,
                       o_ref, cstate_ref, chunkv_ref, glu_ref):
    bt = SAMPLE_BATCH_TILE
    hist = CONV_K - 1
    h = h_ref[...]
    n = _rms(h, gpre_ref[...]).astype(BF16)
    z = _dot(n, win_ref[...])
    u = z[:, 0:SGU_WIDTH]
    v = _ln(z[:, SGU_WIDTH:2 * SGU_WIDTH], slng_ref[...], slnb_ref[...])
    a = z[:, 2 * SGU_WIDTH:2 * SGU_WIDTH + CONV_WIDTH]
    gate = z[:, 2 * SGU_WIDTH + CONV_WIDTH:]
    chunkv_ref[...] = v

    v3 = v.reshape(bt, dec_seq, SGU_WIDTH)
    s = jnp.broadcast_to(bsx_ref[...][None], (bt, dec_seq, SGU_WIDTH))
    for j in range(dec_seq):
        s = s + wsx_ref[j][None] * v3[:, j:j + 1, :]
    out_a = u * s.reshape(bt * dec_seq, SGU_WIDTH)

    glu = a * _sigmoid(gate)
    glu3 = glu.reshape(bt, dec_seq, CONV_WIDTH)
    new_part = jnp.broadcast_to(cb_ref[...][None], (bt, dec_seq, CONV_WIDTH))
    for r in range(dec_seq):
        new_part = new_part + cwx_ref[hist + r][None] * glu3[:, r:r + 1, :]
    old_parts = []
    for b in range(bt):
        acc = cwx_ref[0] * st_ref[0, b:b + 1, :]
        for r in range(1, hist):
            acc = acc + cwx_ref[r] * st_ref[r, b:b + 1, :]
        old_parts.append(acc)
    conv = new_part.reshape(bt * dec_seq, CONV_WIDTH) + jnp.concatenate(old_parts, axis=0)
    out_b = _silu(_ln(conv, clng_ref[...], clnb_ref[...]))

    cstate_ref[0:hist - dec_seq, :, :] = st_ref[dec_seq:hist, :, :]
    for sl in range(CONV_WIDTH // LANES):
        glu_ref[sl] = glu[:, sl * LANES:(sl + 1) * LANES]
    for tt in range(dec_seq):
        for sl in range(CONV_WIDTH // LANES):
            cstate_ref[hist - dec_seq + tt, :, sl * LANES:(sl + 1) * LANES] = (
                glu_ref[sl, pl.ds(tt, bt, stride=dec_seq), :])

    mixed = jnp.concatenate([out_a.astype(BF16), out_b.astype(BF16)], axis=-1)
    h2 = h + _rms(_dot(mixed, wout_ref[...]), gpost_ref[...])

    q = _dot(_rms(h2, xgpre_ref[...]).astype(BF16), wq_ref[...])
    scores = []
    for b in range(bt):
        for hd in range(X_HEADS):
            qbh = q[b * dec_seq:(b + 1) * dec_seq, hd * X_HEAD_DIM:(hd + 1) * X_HEAD_DIM]
            scores.append(_dot_nt(qbh.astype(BF16), _cached_head(k_ref, b, hd)))
    sc = jnp.concatenate(scores, axis=0) * (X_HEAD_DIM ** -0.5)
    e = jnp.exp(sc - jnp.max(sc, axis=-1, keepdims=True))
    pr = e / jnp.sum(e, axis=-1, keepdims=True)
    atts = []
    for b in range(bt):
        heads = []
        for hd in range(X_HEADS):
            i0 = (b * X_HEADS + hd) * dec_seq
            heads.append(_dot(pr[i0:i0 + dec_seq, :].astype(BF16), _cached_head(v_ref, b, hd)))
        atts.append(jnp.concatenate(heads, axis=-1))
    att = jnp.concatenate(atts, axis=0).astype(BF16)
    o_ref[...] = h2 + _rms(_dot(att, wo_ref[...]), xgpost_ref[...])


def _mix_sample(h, state, k, v, p, dec_seq):
    n_rows = h.shape[0]
    bsz = n_rows // dec_seq
    bt = SAMPLE_BATCH_TILE
    hist = CONV_K - 1
    tok = pl.BlockSpec((bt * dec_seq, D_MODEL), lambda i: (i, 0))
    mem = pl.BlockSpec((bt,) + k.shape[1:], lambda i: (i, 0, 0))
    st = pl.BlockSpec((hist, bt, CONV_WIDTH), lambda i: (0, i, 0))
    return pl.pallas_call(
        functools.partial(_mix_sample_kernel, dec_seq),
        out_shape=(jax.ShapeDtypeStruct((n_rows, D_MODEL), F32),
                   jax.ShapeDtypeStruct((hist, bsz, CONV_WIDTH), F32),
                   jax.ShapeDtypeStruct((n_rows, SGU_WIDTH), F32)),
        grid=(bsz // bt,),
        in_specs=[
            tok, st, mem, mem,
            _const_spec((1, D_MODEL)),
            _const_spec((D_MODEL, 2 * SGU_WIDTH + 2 * CONV_WIDTH)),
            _const_spec((1, SGU_WIDTH)), _const_spec((1, SGU_WIDTH)),
            _const_spec((dec_seq, dec_seq, SGU_WIDTH)),
            _const_spec((dec_seq, SGU_WIDTH)),
            _const_spec((hist + dec_seq, dec_seq, CONV_WIDTH)), _const_spec((1, CONV_WIDTH)),
            _const_spec((1, CONV_WIDTH)), _const_spec((1, CONV_WIDTH)),
            _const_spec((D_MODEL, D_MODEL)), _const_spec((1, D_MODEL)),
            _const_spec((1, D_MODEL)),
            _const_spec((D_MODEL, D_MODEL)), _const_spec((D_MODEL, D_MODEL)),
            _const_spec((1, D_MODEL)),
        ],
        out_specs=(tok, st, pl.BlockSpec((bt * dec_seq, SGU_WIDTH), lambda i: (i, 0))),
        scratch_shapes=[pltpu.VMEM((CONV_WIDTH // LANES, bt * dec_seq, LANES), F32)],
        compiler_params=pltpu.CompilerParams(
            dimension_semantics=("arbitrary",), vmem_limit_bytes=VMEM_LIMIT),
        name="mix_sample",
    )(h, state, k, v, p["g_pre"], p["w_in"], p["sln_g"], p["sln_b"], p["w_s_rows"], p["b_s_rows"],
      p["conv_w_rows"], p["conv_b"], p["cln_g"], p["cln_b"], p["w_out"], p["g_post"],
      p["xg_pre"], p["w_q"], p["w_o"], p["xg_post"])


def _head_tiled_rows(c):
    b, m, hh, dh = c.shape
    c = c.reshape(b, m, hh, dh // LANES, LANES)
    return jnp.transpose(c, (0, 1, 3, 2, 4)).reshape(b, m * hh * (dh // LANES), LANES)


def _from_head_tiled_rows(t):
    b = t.shape[0]
    halves = X_HEAD_DIM // LANES
    t = t.reshape(b, MEM_TOKENS, halves, X_HEADS, LANES)
    return jnp.transpose(t, (0, 1, 3, 2, 4)).reshape(b, MEM_TOKENS, X_HEADS, X_HEAD_DIM)


def _row(x):
    return x.reshape(1, -1)


def kernel(x_prompt, x_sample, mem_prompt, cache_mem_k, cache_mem_v, state_conv, ffn1_g_pre, ffn1_w_gate, ffn1_w_up, ffn1_w_down, ffn1_g_post, mix_g_pre, mix_w_in, sgu_ln_g, sgu_ln_b, sgu_w_s, sgu_b_s, conv_w, conv_b, conv_ln_g, conv_ln_b, mix_w_out, mix_g_post, mem_g, xattn_g_pre, xattn_w_q, xattn_w_k, xattn_w_v, xattn_w_o, xattn_g_post, ffn2_g_pre, ffn2_w_gate, ffn2_w_up, ffn2_w_down, ffn2_g_post):
    depth = ffn1_g_pre.shape[0]
    bsz, seq, _ = x_prompt.shape
    dbsz, dec_seq, _ = x_sample.shape
    hist = CONV_K - 1
    assert seq % MIX_ROWS == 0 and dec_seq <= CHUNK and dec_seq <= hist

    hp = x_prompt.reshape(bsz * seq, D_MODEL)
    hs = x_sample.reshape(dbsz * dec_seq, D_MODEL)
    mem = mem_prompt.reshape(bsz * MEM_TOKENS, D_MODEL)
    outs = [[] for _ in range(6)]
    for l in range(depth):
        ffn1 = (_row(ffn1_g_pre[l]), ffn1_w_gate[l], ffn1_w_up[l], ffn1_w_down[l], _row(ffn1_g_post[l]))
        w_s, b_s, cw = sgu_w_s[l], sgu_b_s[l], conv_w[l]
        tril = jnp.tril(w_s[:, :dec_seq, :dec_seq])
        w_s_rows = jnp.repeat(jnp.transpose(tril, (2, 1, 0)), SGU_HEAD_DIM, axis=2)
        b_s_rows = jnp.repeat(b_s[:, :dec_seq].T, SGU_HEAD_DIM, axis=1)
        tap = jnp.arange(hist + dec_seq)[:, None] - jnp.arange(dec_seq)[None, :]
        conv_w_rows = jnp.where(((tap >= 0) & (tap < CONV_K))[:, :, None],
                                cw[jnp.clip(tap, 0, CONV_K - 1)], 0.0)
        p = dict(
            g_pre=_row(mix_g_pre[l]),
            sln_g=_row(sgu_ln_g[l]), sln_b=_row(sgu_ln_b[l]),
            w_s=w_s, b_s_full=jnp.repeat(b_s.T, SGU_HEAD_DIM, axis=1),
            w_s_rows=w_s_rows, b_s_rows=b_s_rows,
            conv_w=cw, conv_w_rows=conv_w_rows, conv_b=_row(conv_b[l]),
            cln_g=_row(conv_ln_g[l]), cln_b=_row(conv_ln_b[l]),
            g_post=_row(mix_g_post[l]), xg_pre=_row(xattn_g_pre[l]), xg_post=_row(xattn_g_post[l]),
        )

        k_t, v_t, k_b, v_b = _mem_kv(mem, _row(mem_g[l]), xattn_w_k[l].astype(BF16),
                                     xattn_w_v[l].astype(BF16))

        later = (ffn2_w_gate[l], ffn2_w_up[l], ffn2_w_down[l],
                 mix_w_in[l], mix_w_out[l], xattn_w_q[l], xattn_w_o[l])
        hp, hs, w2_gate, w2_up, w2_down, p["w_in"], p["w_out"], p["w_q"], p["w_o"] = _ffn(
            hp, hs, *ffn1, to_cast=later)
        hp, cs_p, cv_p = _mix_prompt(hp, k_b, v_b, p, seq // MIX_ROWS)
        hs, cs_s, cv_s = _mix_sample(hs, jnp.transpose(state_conv[l], (1, 0, 2)),
                                     _head_tiled_rows(cache_mem_k[l]),
                                     _head_tiled_rows(cache_mem_v[l]), p, dec_seq)
        hp, hs = _ffn(hp, hs, _row(ffn2_g_pre[l]), w2_gate, w2_up, w2_down, _row(ffn2_g_post[l]))

        outs[0].append(_from_head_tiled_rows(k_t))
        outs[1].append(_from_head_tiled_rows(v_t))
        outs[2].append(cs_p)
        outs[3].append(jnp.transpose(cs_s, (1, 0, 2)))
        outs[4].append(cv_p)
        outs[5].append(cv_s.reshape(dbsz, dec_seq, SGU_WIDTH))

    return (hp.reshape(bsz, seq, D_MODEL), hs.reshape(dbsz, dec_seq, D_MODEL),
            jnp.stack(outs[0]), jnp.stack(outs[1]), jnp.stack(outs[2]), jnp.stack(outs[3]),
            jnp.stack(outs[4]), jnp.stack(outs[5]))
```

```python
import functools

import jax
import jax.numpy as jnp
from jax import lax
from jax.experimental import pallas as pl
from jax.experimental.pallas import tpu as pltpu

D_MODEL = 1024
SGU_WIDTH = 512
CONV_WIDTH = 512
SGU_HEADS = 4
SGU_HEAD_DIM = 128
CHUNK = 128
CONV_K = 31
MEM_TOKENS = 256
X_HEADS = 4
X_HEAD_DIM = 256
FFN_DIM = 4096
EPS = 1e-6
LANES = 128
SUBLANES = 8

F32 = jnp.float32
BF16 = jnp.bfloat16

FFN_ROWS = 512
FFN_COLS = 1024
MIX_ROWS = 512
SAMPLE_BATCH_TILE = 16
CARRY_ROWS = 32
CONV_BLOCK_ROWS = 128
VMEM_LIMIT = 56 * 1024 * 1024
FFN_VMEM_LIMIT = 60 * 1024 * 1024
CAST_SLABS = 32
BF16_SUBLANES = 16
STAGE_ROWS = 256
STAGE_COLS = 1024
STAGE_SLOTS = 4


def _rms(x, g):
    return x * lax.rsqrt(jnp.mean(x * x, axis=-1, keepdims=True) + EPS) * g


def _ln(x, g, b):
    mu = jnp.mean(x, axis=-1, keepdims=True)
    xc = x - mu
    return xc * lax.rsqrt(jnp.mean(xc * xc, axis=-1, keepdims=True) + EPS) * g + b


def _sigmoid(x):
    return 0.5 + 0.5 * jnp.tanh(0.5 * x)


def _silu(x):
    half = 0.5 * x
    return half + half * jnp.tanh(half)


def _dot(a, b):
    return jnp.dot(a, b, preferred_element_type=F32)


def _dot_nt(a, b):
    return lax.dot_general(a, b, (((1,), (1,)), ((), ())), preferred_element_type=F32)


def _const_spec(shape):
    nd = len(shape)
    return pl.BlockSpec(shape, lambda *_: (0,) * nd, pipeline_mode=pl.Buffered(1))


def _ffn_tile(x_ref, gpre_ref, wg_ref, wu_ref, wd_ref, gpost_ref, o_ref, acc_ref):
    x = x_ref[...]
    n = _rms(x, gpre_ref[...]).astype(BF16)
    for c in range(FFN_DIM // FFN_COLS):
        cols = slice(c * FFN_COLS, (c + 1) * FFN_COLS)
        g = _dot(n, wg_ref[:, cols])
        u = _dot(n, wu_ref[:, cols])
        h = (_silu(g) * u).astype(BF16)
        d = _dot(h, wd_ref[cols, :])
        if c == 0:
            acc_ref[...] = d
        else:
            acc_ref[...] += d
    o_ref[...] = x + 0.5 * _rms(acc_ref[...], gpost_ref[...])


def _stage_and_convert(w_hbm, w_ref, stage_ref, sems):
    pieces = [(r0, c0) for r0 in range(0, w_hbm.shape[0], STAGE_ROWS)
              for c0 in range(0, w_hbm.shape[1], STAGE_COLS)]

    def copy(idx):
        r0, c0 = pieces[idx]
        slot = idx % STAGE_SLOTS
        return pltpu.make_async_copy(w_hbm.at[pl.ds(r0, STAGE_ROWS), pl.ds(c0, STAGE_COLS)],
                                     stage_ref.at[slot], sems.at[slot])

    for idx in range(min(STAGE_SLOTS, len(pieces))):
        copy(idx).start()
    for idx, (r0, c0) in enumerate(pieces):
        copy(idx).wait()
        w_ref[r0:r0 + STAGE_ROWS, c0:c0 + STAGE_COLS] = stage_ref[idx % STAGE_SLOTS].astype(BF16)
        if idx + STAGE_SLOTS < len(pieces):
            copy(idx + STAGE_SLOTS).start()


def _ffn_kernel(n_a, n_cast, own_f32, xa_ref, xb_ref, gpre_ref, wg_ref, wu_ref, wd_ref, gpost_ref, *refs):
    cast_src = refs[:n_cast]
    oa_ref, ob_ref = refs[n_cast:n_cast + 2]
    cast_dst = refs[n_cast + 2:2 * n_cast + 2]
    acc_ref = refs[2 * n_cast + 2]
    i = pl.program_id(0)

    if own_f32:
        wg_hbm, wu_hbm, wd_hbm = wg_ref, wu_ref, wd_ref
        wg_ref, wu_ref, wd_ref, stage_ref, sems = refs[2 * n_cast + 3:]

        @pl.when(i == 0)
        def _():
            for w_hbm, w_ref in ((wg_hbm, wg_ref), (wu_hbm, wu_ref), (wd_hbm, wd_ref)):
                _stage_and_convert(w_hbm, w_ref, stage_ref, sems)

    for src, dst in zip(cast_src, cast_dst):
        dst[...] = src[...].astype(BF16)

    @pl.when(i < n_a)
    def _():
        _ffn_tile(xa_ref, gpre_ref, wg_ref, wu_ref, wd_ref, gpost_ref, oa_ref, acc_ref)

    @pl.when(i >= n_a)
    def _():
        _ffn_tile(xb_ref, gpre_ref, wg_ref, wu_ref, wd_ref, gpost_ref, ob_ref, acc_ref)


def _ffn(xa, xb, g_pre, w_gate, w_up, w_down, g_post, to_cast=()):
    n_a = xa.shape[0] // FFN_ROWS
    n_b = xb.shape[0] // FFN_ROWS
    own_f32 = w_gate.dtype == F32
    assert w_up.dtype == w_gate.dtype and w_down.dtype == w_gate.dtype
    spec_a = pl.BlockSpec((FFN_ROWS, D_MODEL), lambda i: (jnp.minimum(i, n_a - 1), 0))
    spec_b = pl.BlockSpec((FFN_ROWS, D_MODEL), lambda i: (jnp.maximum(i - n_a, 0), 0))
    n_slabs = CAST_SLABS
    assert n_slabs <= n_a + n_b and all(w.shape[0] % (n_slabs * BF16_SUBLANES) == 0 for w in to_cast)
    cast_specs = [pl.BlockSpec((w.shape[0] // n_slabs, w.shape[1]),
                               lambda i: (jnp.minimum(i, n_slabs - 1), 0)) for w in to_cast]
    scratch = [pltpu.VMEM((FFN_ROWS, D_MODEL), F32)]
    if own_f32:
        weight_specs = [pl.BlockSpec(memory_space=pl.ANY)] * 3
        scratch += [pltpu.VMEM((D_MODEL, FFN_DIM), BF16), pltpu.VMEM((D_MODEL, FFN_DIM), BF16),
                    pltpu.VMEM((FFN_DIM, D_MODEL), BF16),
                    pltpu.VMEM((STAGE_SLOTS, STAGE_ROWS, STAGE_COLS), F32),
                    pltpu.SemaphoreType.DMA((STAGE_SLOTS,))]
    else:
        weight_specs = [_const_spec((D_MODEL, FFN_DIM)), _const_spec((D_MODEL, FFN_DIM)),
                        _const_spec((FFN_DIM, D_MODEL))]
    return pl.pallas_call(
        functools.partial(_ffn_kernel, n_a, len(to_cast), own_f32),
        out_shape=(jax.ShapeDtypeStruct(xa.shape, F32), jax.ShapeDtypeStruct(xb.shape, F32),
                   *[jax.ShapeDtypeStruct(w.shape, BF16) for w in to_cast]),
        grid=(n_a + n_b,),
        in_specs=[
            spec_a, spec_b,
            _const_spec((1, D_MODEL)),
            *weight_specs,
            _const_spec((1, D_MODEL)),
            *cast_specs,
        ],
        out_specs=(spec_a, spec_b, *cast_specs),
        scratch_shapes=scratch,
        compiler_params=pltpu.CompilerParams(
            dimension_semantics=("arbitrary",), vmem_limit_bytes=FFN_VMEM_LIMIT),
        name="ffn",
    )(xa, xb, g_pre, w_gate, w_up, w_down, g_post, *to_cast)


def _mem_kv_kernel(m_ref, g_ref, wk_ref, wv_ref, kt_ref, vt_ref, kb_ref, vb_ref):
    mn = _rms(m_ref[...], g_ref[...]).astype(BF16)
    halves = X_HEAD_DIM // LANES
    for w_ref, t_ref, b_ref in ((wk_ref, kt_ref, kb_ref), (wv_ref, vt_ref, vb_ref)):
        x = _dot(mn, w_ref[...])
        b_ref[...] = x.astype(BF16)
        for hd in range(X_HEADS):
            for j in range(halves):
                c0 = hd * X_HEAD_DIM + j * LANES
                t_ref[pl.ds(j * X_HEADS + hd, MEM_TOKENS, stride=X_HEADS * halves), :] = x[:, c0:c0 + LANES]


def _mem_kv(mem, g, w_k, w_v):
    bsz = mem.shape[0] // MEM_TOKENS
    tiled_rows = MEM_TOKENS * D_MODEL // LANES
    tiled = jax.ShapeDtypeStruct((bsz, tiled_rows, LANES), F32)
    dense = jax.ShapeDtypeStruct((bsz, MEM_TOKENS, D_MODEL), BF16)
    tiled_spec = pl.BlockSpec((None, tiled_rows, LANES), lambda i: (i, 0, 0))
    dense_spec = pl.BlockSpec((None, MEM_TOKENS, D_MODEL), lambda i: (i, 0, 0))
    return pl.pallas_call(
        _mem_kv_kernel,
        out_shape=(tiled, tiled, dense, dense),
        grid=(bsz,),
        in_specs=[
            pl.BlockSpec((MEM_TOKENS, D_MODEL), lambda i: (i, 0)),
            _const_spec((1, D_MODEL)),
            _const_spec((D_MODEL, D_MODEL)),
            _const_spec((D_MODEL, D_MODEL)),
        ],
        out_specs=(tiled_spec, tiled_spec, dense_spec, dense_spec),
        compiler_params=pltpu.CompilerParams(
            dimension_semantics=("arbitrary",), vmem_limit_bytes=VMEM_LIMIT),
        name="mem_kv",
    )(mem, g, w_k, w_v)


def _attend(q, k, v):
    outs = []
    for hd in range(X_HEADS):
        cols = slice(hd * X_HEAD_DIM, (hd + 1) * X_HEAD_DIM)
        s = _dot_nt(q[:, cols].astype(BF16), k[:, cols]) * (X_HEAD_DIM ** -0.5)
        e = jnp.exp(s - jnp.max(s, axis=-1, keepdims=True))
        p = e / jnp.sum(e, axis=-1, keepdims=True)
        outs.append(_dot(p.astype(BF16), v[:, cols]).astype(BF16))
    return jnp.concatenate(outs, axis=-1)


def _conv_block(cbuf_ref, cw_ref, base, r0, c0):
    nrow = CONV_BLOCK_ROWS
    lanes = slice(c0, c0 + LANES)
    y = None
    for res in range(SUBLANES):
        taps = [k for k in range(CONV_K) if (base + k) % SUBLANES == res]
        ext = nrow + (SUBLANES if res else 0)
        part = None
        for k in taps:
            start = r0 + base + k - res
            term = cbuf_ref[start:start + ext, lanes] * cw_ref[k:k + 1, lanes]
            part = term if part is None else part + term
        part = part[res:res + nrow, :]
        y = part if y is None else y + part
    return y


def _mix_prompt_kernel(tiles_per_seq, dec_seq, h_ref, hprev_ref, k_ref, v_ref, sq_ref, sk_ref, sv_ref,
                       gpre_ref, win_ref, slng_ref, slnb_ref,
                       ws_ref, bs_ref, cw_ref, cb_ref, clng_ref, clnb_ref, wout_ref, gpost_ref,
                       xgpre_ref, wq_ref, wo_ref, xgpost_ref,
                       o_ref, cstate_ref, chunkv_ref, satt_ref, cbuf_ref, mixed_ref, res_ref):
    s = pl.program_id(0)
    rows = MIX_ROWS
    tile = jnp.minimum(s, pl.num_programs(0) - 2)
    last_of_seq = (tile % tiles_per_seq) == tiles_per_seq - 1

    @pl.when(s == 0)
    def _():
        cbuf_ref[0:CARRY_ROWS, :] = jnp.zeros((CARRY_ROWS, CONV_WIDTH), F32)
        mixed_ref[...] = jnp.zeros((rows, D_MODEL), BF16)

    n = _rms(h_ref[...], gpre_ref[...]).astype(BF16)
    z_conv = _dot(n, win_ref[:, 2 * SGU_WIDTH:])
    a = z_conv[:, 0:CONV_WIDTH]
    gate = z_conv[:, CONV_WIDTH:]
    cbuf_ref[CARRY_ROWS:CARRY_ROWS + rows, :] = a * _sigmoid(gate)
    z_sgu = _dot(n, win_ref[:, 0:2 * SGU_WIDTH])
    u = z_sgu[:, 0:SGU_WIDTH]
    v = _ln(z_sgu[:, SGU_WIDTH:], slng_ref[...], slnb_ref[...])
    chunkv_ref[...] = v[rows - CHUNK:, :]

    res_ref[...] = hprev_ref[...] + _rms(_dot(mixed_ref[...], wout_ref[...]), gpost_ref[...])
    q = _dot(_rms(res_ref[...], xgpre_ref[...]).astype(BF16), wq_ref[...])

    row_id = lax.broadcasted_iota(jnp.int32, (CHUNK, CHUNK), 0)
    col_id = lax.broadcasted_iota(jnp.int32, (CHUNK, CHUNK), 1)
    vb = v.astype(BF16)
    s_heads = []
    for hd in range(SGU_HEADS):
        ws = jnp.where(row_id >= col_id, ws_ref[hd], 0.0).astype(BF16)
        cols = slice(hd * SGU_HEAD_DIM, (hd + 1) * SGU_HEAD_DIM)
        s_chunks = [_dot(ws, vb[c * CHUNK:(c + 1) * CHUNK, cols]) for c in range(rows // CHUNK)]
        s_heads.append(jnp.concatenate(s_chunks, axis=0))
    bias = jnp.concatenate([bs_ref[...]] * (rows // CHUNK), axis=0)
    out_a = u * (jnp.concatenate(s_heads, axis=-1) + bias)

    att = _attend(q, k_ref[...], v_ref[...])
    o_ref[...] = res_ref[...] + _rms(_dot(att, wo_ref[...]), xgpost_ref[...])

    satt_ref[...] = _cached_attention(sq_ref[...], sk_ref, sv_ref, dec_seq)

    base = CARRY_ROWS - (CONV_K - 1)
    conv = jnp.concatenate(
        [jnp.concatenate([_conv_block(cbuf_ref, cw_ref, base, r0, c0)
                          for c0 in range(0, CONV_WIDTH, LANES)], axis=-1)
         for r0 in range(0, rows, CONV_BLOCK_ROWS)], axis=0) + cb_ref[...]
    out_b = _silu(_ln(conv, clng_ref[...], clnb_ref[...]))
    cstate_ref[...] = cbuf_ref[rows + base:rows + CARRY_ROWS, :]
    cbuf_ref[0:CARRY_ROWS, :] = jnp.where(last_of_seq, 0.0, cbuf_ref[rows:rows + CARRY_ROWS, :])
    mixed_ref[...] = jnp.concatenate([out_a.astype(BF16), out_b.astype(BF16)], axis=-1)


def _mix_prompt(h, k, v, sq, sk, sv, p, tiles_per_seq, dec_seq):
    n_tiles = h.shape[0] // MIX_ROWS
    bsz = n_tiles // tiles_per_seq
    mix_tile = lambda s: jnp.minimum(s, n_tiles - 1)
    att_tile = lambda s: jnp.maximum(s - 1, 0)
    mem = pl.BlockSpec((None, MEM_TOKENS, D_MODEL), lambda s: (att_tile(s) // tiles_per_seq, 0, 0))
    slab = sk.shape[0] // n_tiles
    assert slab * n_tiles == sk.shape[0] and (slab * dec_seq) % SUBLANES == 0
    slab_step = lambda s: jnp.minimum(s, n_tiles - 1)
    sq_spec = pl.BlockSpec((slab * dec_seq, D_MODEL), lambda s: (slab_step(s), 0))
    cache_spec = pl.BlockSpec((slab,) + sk.shape[1:], lambda s: (slab_step(s), 0, 0))
    return pl.pallas_call(
        functools.partial(_mix_prompt_kernel, tiles_per_seq, dec_seq),
        out_shape=(jax.ShapeDtypeStruct(h.shape, F32),
                   jax.ShapeDtypeStruct((bsz, CONV_K - 1, CONV_WIDTH), F32),
                   jax.ShapeDtypeStruct((bsz, CHUNK, SGU_WIDTH), F32),
                   jax.ShapeDtypeStruct(sq.shape, F32)),
        grid=(n_tiles + 1,),
        in_specs=[
            pl.BlockSpec((MIX_ROWS, D_MODEL), lambda s: (mix_tile(s), 0)),
            pl.BlockSpec((MIX_ROWS, D_MODEL), lambda s: (att_tile(s), 0)), mem, mem,
            sq_spec, cache_spec, cache_spec,
            _const_spec((1, D_MODEL)),
            _const_spec((D_MODEL, 2 * SGU_WIDTH + 2 * CONV_WIDTH)),
            _const_spec((1, SGU_WIDTH)), _const_spec((1, SGU_WIDTH)),
            _const_spec((SGU_HEADS, CHUNK, CHUNK)),
            _const_spec((CHUNK, SGU_WIDTH)),
            _const_spec((CONV_K, CONV_WIDTH)), _const_spec((1, CONV_WIDTH)),
            _const_spec((1, CONV_WIDTH)), _const_spec((1, CONV_WIDTH)),
            _const_spec((D_MODEL, D_MODEL)), _const_spec((1, D_MODEL)),
            _const_spec((1, D_MODEL)),
            _const_spec((D_MODEL, D_MODEL)), _const_spec((D_MODEL, D_MODEL)),
            _const_spec((1, D_MODEL)),
        ],
        out_specs=(pl.BlockSpec((MIX_ROWS, D_MODEL), lambda s: (att_tile(s), 0)),
                   pl.BlockSpec((None, CONV_K - 1, CONV_WIDTH),
                                lambda s: (mix_tile(s) // tiles_per_seq, 0, 0)),
                   pl.BlockSpec((None, CHUNK, SGU_WIDTH),
                                lambda s: (mix_tile(s) // tiles_per_seq, 0, 0)),
                   sq_spec),
        scratch_shapes=[pltpu.VMEM((MIX_ROWS + CARRY_ROWS, CONV_WIDTH), F32),
                        pltpu.VMEM((MIX_ROWS, D_MODEL), BF16),
                        pltpu.VMEM((MIX_ROWS, D_MODEL), F32)],
        compiler_params=pltpu.CompilerParams(
            dimension_semantics=("arbitrary",), vmem_limit_bytes=VMEM_LIMIT),
        name="mix_prompt",
    )(h, h, k, v, sq, sk, sv, p["g_pre"], p["w_in"], p["sln_g"], p["sln_b"], p["w_s"], p["b_s_full"],
      p["conv_w"], p["conv_b"], p["cln_g"], p["cln_b"], p["w_out"], p["g_post"],
      p["xg_pre"], p["w_q"], p["w_o"], p["xg_post"])


def _cached_head(c_ref, b, hd):
    halves = X_HEAD_DIM // LANES
    stride = X_HEADS * halves
    parts = [c_ref[b, pl.ds(j * X_HEADS + hd, MEM_TOKENS, stride=stride), :] for j in range(halves)]
    return jnp.concatenate(parts, axis=-1).astype(BF16)


def _cached_attention(q, k_ref, v_ref, dec_seq):
    bt = k_ref.shape[0]
    scores = []
    for b in range(bt):
        for hd in range(X_HEADS):
            qbh = q[b * dec_seq:(b + 1) * dec_seq, hd * X_HEAD_DIM:(hd + 1) * X_HEAD_DIM]
            scores.append(_dot_nt(qbh.astype(BF16), _cached_head(k_ref, b, hd)))
    sc = jnp.concatenate(scores, axis=0) * (X_HEAD_DIM ** -0.5)
    e = jnp.exp(sc - jnp.max(sc, axis=-1, keepdims=True))
    pr = e / jnp.sum(e, axis=-1, keepdims=True)
    atts = []
    for b in range(bt):
        heads = []
        for hd in range(X_HEADS):
            i0 = (b * X_HEADS + hd) * dec_seq
            heads.append(_dot(pr[i0:i0 + dec_seq, :].astype(BF16), _cached_head(v_ref, b, hd)))
        atts.append(jnp.concatenate(heads, axis=-1))
    return jnp.concatenate(atts, axis=0)


def _sample_pre_kernel(dec_seq, h_ref, st_ref, gpre_ref, win_ref, slng_ref, slnb_ref,
                       wsx_ref, bsx_ref, cwx_ref, cb_ref, clng_ref, clnb_ref, wout_ref, gpost_ref,
                       xgpre_ref, wq_ref,
                       h2_ref, q_ref, cstate_ref, chunkv_ref, glu_ref):
    bt = SAMPLE_BATCH_TILE
    hist = CONV_K - 1
    h = h_ref[...]
    n = _rms(h, gpre_ref[...]).astype(BF16)
    z = _dot(n, win_ref[...])
    u = z[:, 0:SGU_WIDTH]
    v = _ln(z[:, SGU_WIDTH:2 * SGU_WIDTH], slng_ref[...], slnb_ref[...])
    a = z[:, 2 * SGU_WIDTH:2 * SGU_WIDTH + CONV_WIDTH]
    gate = z[:, 2 * SGU_WIDTH + CONV_WIDTH:]
    chunkv_ref[...] = v

    v3 = v.reshape(bt, dec_seq, SGU_WIDTH)
    s = jnp.broadcast_to(bsx_ref[...][None], (bt, dec_seq, SGU_WIDTH))
    for j in range(dec_seq):
        s = s + wsx_ref[j][None] * v3[:, j:j + 1, :]
    out_a = u * s.reshape(bt * dec_seq, SGU_WIDTH)

    glu = a * _sigmoid(gate)
    glu3 = glu.reshape(bt, dec_seq, CONV_WIDTH)
    new_part = jnp.broadcast_to(cb_ref[...][None], (bt, dec_seq, CONV_WIDTH))
    for r in range(dec_seq):
        new_part = new_part + cwx_ref[hist + r][None] * glu3[:, r:r + 1, :]
    old_parts = []
    for b in range(bt):
        acc = cwx_ref[0] * st_ref[0, b:b + 1, :]
        for r in range(1, hist):
            acc = acc + cwx_ref[r] * st_ref[r, b:b + 1, :]
        old_parts.append(acc)
    conv = new_part.reshape(bt * dec_seq, CONV_WIDTH) + jnp.concatenate(old_parts, axis=0)
    out_b = _silu(_ln(conv, clng_ref[...], clnb_ref[...]))

    cstate_ref[0:hist - dec_seq, :, :] = st_ref[dec_seq:hist, :, :]
    for sl in range(CONV_WIDTH // LANES):
        glu_ref[sl] = glu[:, sl * LANES:(sl + 1) * LANES]
    for tt in range(dec_seq):
        for sl in range(CONV_WIDTH // LANES):
            cstate_ref[hist - dec_seq + tt, :, sl * LANES:(sl + 1) * LANES] = (
                glu_ref[sl, pl.ds(tt, bt, stride=dec_seq), :])

    mixed = jnp.concatenate([out_a.astype(BF16), out_b.astype(BF16)], axis=-1)
    h2 = h + _rms(_dot(mixed, wout_ref[...]), gpost_ref[...])
    h2_ref[...] = h2
    q_ref[...] = _dot(_rms(h2, xgpre_ref[...]).astype(BF16), wq_ref[...])


def _sample_pre(h, state, p, dec_seq):
    n_rows = h.shape[0]
    bsz = n_rows // dec_seq
    bt = SAMPLE_BATCH_TILE
    hist = CONV_K - 1
    tok = pl.BlockSpec((bt * dec_seq, D_MODEL), lambda i: (i, 0))
    st = pl.BlockSpec((hist, bt, CONV_WIDTH), lambda i: (0, i, 0))
    return pl.pallas_call(
        functools.partial(_sample_pre_kernel, dec_seq),
        out_shape=(jax.ShapeDtypeStruct((n_rows, D_MODEL), F32),
                   jax.ShapeDtypeStruct((n_rows, D_MODEL), F32),
                   jax.ShapeDtypeStruct((hist, bsz, CONV_WIDTH), F32),
                   jax.ShapeDtypeStruct((n_rows, SGU_WIDTH), F32)),
        grid=(bsz // bt,),
        in_specs=[
            tok, st,
            _const_spec((1, D_MODEL)),
            _const_spec((D_MODEL, 2 * SGU_WIDTH + 2 * CONV_WIDTH)),
            _const_spec((1, SGU_WIDTH)), _const_spec((1, SGU_WIDTH)),
            _const_spec((dec_seq, dec_seq, SGU_WIDTH)),
            _const_spec((dec_seq, SGU_WIDTH)),
            _const_spec((hist + dec_seq, dec_seq, CONV_WIDTH)), _const_spec((1, CONV_WIDTH)),
            _const_spec((1, CONV_WIDTH)), _const_spec((1, CONV_WIDTH)),
            _const_spec((D_MODEL, D_MODEL)), _const_spec((1, D_MODEL)),
            _const_spec((1, D_MODEL)),
            _const_spec((D_MODEL, D_MODEL)),
        ],
        out_specs=(tok, tok, st, pl.BlockSpec((bt * dec_seq, SGU_WIDTH), lambda i: (i, 0))),
        scratch_shapes=[pltpu.VMEM((CONV_WIDTH // LANES, bt * dec_seq, LANES), F32)],
        compiler_params=pltpu.CompilerParams(
            dimension_semantics=("arbitrary",), vmem_limit_bytes=VMEM_LIMIT),
        name="sample_pre",
    )(h, state, p["g_pre"], p["w_in"], p["sln_g"], p["sln_b"], p["w_s_rows"], p["b_s_rows"],
      p["conv_w_rows"], p["conv_b"], p["cln_g"], p["cln_b"], p["w_out"], p["g_post"],
      p["xg_pre"], p["w_q"])


def _sample_post_kernel(h2_ref, att_ref, wo_ref, g_ref, o_ref):
    o_ref[...] = h2_ref[...] + _rms(_dot(att_ref[...].astype(BF16), wo_ref[...]), g_ref[...])


def _sample_post(h2, att, p):
    rows = min(h2.shape[0], MIX_ROWS)
    tok = pl.BlockSpec((rows, D_MODEL), lambda i: (i, 0))
    return pl.pallas_call(
        _sample_post_kernel,
        out_shape=jax.ShapeDtypeStruct(h2.shape, F32),
        grid=(h2.shape[0] // rows,),
        in_specs=[tok, tok, _const_spec((D_MODEL, D_MODEL)), _const_spec((1, D_MODEL))],
        out_specs=tok,
        compiler_params=pltpu.CompilerParams(
            dimension_semantics=("arbitrary",), vmem_limit_bytes=VMEM_LIMIT),
        name="sample_post",
    )(h2, att, p["w_o"], p["xg_post"])


def _head_tiled_rows(c):
    b, m, hh, dh = c.shape
    c = c.reshape(b, m, hh, dh // LANES, LANES)
    return jnp.transpose(c, (0, 1, 3, 2, 4)).reshape(b, m * hh * (dh // LANES), LANES)


def _from_head_tiled_rows(t):
    b = t.shape[0]
    halves = X_HEAD_DIM // LANES
    t = t.reshape(b, MEM_TOKENS, halves, X_HEADS, LANES)
    return jnp.transpose(t, (0, 1, 3, 2, 4)).reshape(b, MEM_TOKENS, X_HEADS, X_HEAD_DIM)


def _row(x):
    return x.reshape(1, -1)


def kernel(x_prompt, x_sample, mem_prompt, cache_mem_k, cache_mem_v, state_conv, ffn1_g_pre, ffn1_w_gate, ffn1_w_up, ffn1_w_down, ffn1_g_post, mix_g_pre, mix_w_in, sgu_ln_g, sgu_ln_b, sgu_w_s, sgu_b_s, conv_w, conv_b, conv_ln_g, conv_ln_b, mix_w_out, mix_g_post, mem_g, xattn_g_pre, xattn_w_q, xattn_w_k, xattn_w_v, xattn_w_o, xattn_g_post, ffn2_g_pre, ffn2_w_gate, ffn2_w_up, ffn2_w_down, ffn2_g_post):
    depth = ffn1_g_pre.shape[0]
    bsz, seq, _ = x_prompt.shape
    dbsz, dec_seq, _ = x_sample.shape
    hist = CONV_K - 1
    assert seq % MIX_ROWS == 0 and dec_seq <= CHUNK and dec_seq <= hist

    hp = x_prompt.reshape(bsz * seq, D_MODEL)
    hs = x_sample.reshape(dbsz * dec_seq, D_MODEL)
    mem = mem_prompt.reshape(bsz * MEM_TOKENS, D_MODEL)
    outs = [[] for _ in range(6)]
    for l in range(depth):
        ffn1 = (_row(ffn1_g_pre[l]), ffn1_w_gate[l], ffn1_w_up[l], ffn1_w_down[l], _row(ffn1_g_post[l]))
        w_s, b_s, cw = sgu_w_s[l], sgu_b_s[l], conv_w[l]
        tril = jnp.tril(w_s[:, :dec_seq, :dec_seq])
        w_s_rows = jnp.repeat(jnp.transpose(tril, (2, 1, 0)), SGU_HEAD_DIM, axis=2)
        b_s_rows = jnp.repeat(b_s[:, :dec_seq].T, SGU_HEAD_DIM, axis=1)
        tap = jnp.arange(hist + dec_seq)[:, None] - jnp.arange(dec_seq)[None, :]
        conv_w_rows = jnp.where(((tap >= 0) & (tap < CONV_K))[:, :, None],
                                cw[jnp.clip(tap, 0, CONV_K - 1)], 0.0)
        p = dict(
            g_pre=_row(mix_g_pre[l]),
            sln_g=_row(sgu_ln_g[l]), sln_b=_row(sgu_ln_b[l]),
            w_s=w_s, b_s_full=jnp.repeat(b_s.T, SGU_HEAD_DIM, axis=1),
            w_s_rows=w_s_rows, b_s_rows=b_s_rows,
            conv_w=cw, conv_w_rows=conv_w_rows, conv_b=_row(conv_b[l]),
            cln_g=_row(conv_ln_g[l]), cln_b=_row(conv_ln_b[l]),
            g_post=_row(mix_g_post[l]), xg_pre=_row(xattn_g_pre[l]), xg_post=_row(xattn_g_post[l]),
        )

        k_t, v_t, k_b, v_b = _mem_kv(mem, _row(mem_g[l]), xattn_w_k[l].astype(BF16),
                                     xattn_w_v[l].astype(BF16))

        later = (ffn2_w_gate[l], ffn2_w_up[l], ffn2_w_down[l],
                 mix_w_in[l], mix_w_out[l], xattn_w_q[l], xattn_w_o[l])
        hp, hs, w2_gate, w2_up, w2_down, p["w_in"], p["w_out"], p["w_q"], p["w_o"] = _ffn(
            hp, hs, *ffn1, to_cast=later)
        hs, q_s, cs_s, cv_s = _sample_pre(hs, jnp.transpose(state_conv[l], (1, 0, 2)), p, dec_seq)
        hp, cs_p, cv_p, att_s = _mix_prompt(hp, k_b, v_b, q_s, _head_tiled_rows(cache_mem_k[l]),
                                            _head_tiled_rows(cache_mem_v[l]), p, seq // MIX_ROWS, dec_seq)
        hs = _sample_post(hs, att_s, p)
        hp, hs = _ffn(hp, hs, _row(ffn2_g_pre[l]), w2_gate, w2_up, w2_down, _row(ffn2_g_post[l]))

        outs[0].append(_from_head_tiled_rows(k_t))
        outs[1].append(_from_head_tiled_rows(v_t))
        outs[2].append(cs_p)
        outs[3].append(jnp.transpose(cs_s, (1, 0, 2)))
        outs[4].append(cv_p)
        outs[5].append(cv_s.reshape(dbsz, dec_seq, SGU_WIDTH))

    return (hp.reshape(bsz, seq, D_MODEL), hs.reshape(dbsz, dec_seq, D_MODEL),
            jnp.stack(outs[0]), jnp.stack(outs[1]), jnp.stack(outs[2]), jnp.stack(outs[3]),
            jnp.stack(outs[4]), jnp.stack(outs[5]))
```

```python
import functools

import jax
import jax.numpy as jnp
from jax import lax
from jax.experimental import pallas as pl
from jax.experimental.pallas import tpu as pltpu

D_MODEL = 1024
SGU_WIDTH = 512
CONV_WIDTH = 512
SGU_HEADS = 4
SGU_HEAD_DIM = 128
CHUNK = 128
CONV_K = 31
MEM_TOKENS = 256
X_HEADS = 4
X_HEAD_DIM = 256
FFN_DIM = 4096
EPS = 1e-6
LANES = 128
SUBLANES = 8

F32 = jnp.float32
BF16 = jnp.bfloat16

FFN_ROWS = 512
FFN_COLS = 1024
MIX_ROWS = 512
SAMPLE_BATCH_TILE = 16
CARRY_ROWS = 32
CONV_BLOCK_ROWS = 128
VMEM_LIMIT = 56 * 1024 * 1024
FFN_VMEM_LIMIT = 60 * 1024 * 1024
CAST_SLABS = 32
BF16_SUBLANES = 16
STAGE_ROWS = 256
STAGE_COLS = 1024
STAGE_SLOTS = 4


def _rms(x, g):
    return x * lax.rsqrt(jnp.mean(x * x, axis=-1, keepdims=True) + EPS) * g


def _ln(x, g, b):
    mu = jnp.mean(x, axis=-1, keepdims=True)
    xc = x - mu
    return xc * lax.rsqrt(jnp.mean(xc * xc, axis=-1, keepdims=True) + EPS) * g + b


def _sigmoid(x):
    return 0.5 + 0.5 * jnp.tanh(0.5 * x)


def _silu(x):
    half = 0.5 * x
    return half + half * jnp.tanh(half)


def _dot(a, b):
    return jnp.dot(a, b, preferred_element_type=F32)


def _dot_nt(a, b):
    return lax.dot_general(a, b, (((1,), (1,)), ((), ())), preferred_element_type=F32)


def _const_spec(shape):
    nd = len(shape)
    return pl.BlockSpec(shape, lambda *_: (0,) * nd, pipeline_mode=pl.Buffered(1))


def _ffn_tile(x_ref, gpre_ref, wg_ref, wu_ref, wd_ref, gpost_ref, o_ref, acc_ref):
    x = x_ref[...]
    n = _rms(x, gpre_ref[...]).astype(BF16)
    for c in range(FFN_DIM // FFN_COLS):
        cols = slice(c * FFN_COLS, (c + 1) * FFN_COLS)
        g = _dot(n, wg_ref[:, cols])
        u = _dot(n, wu_ref[:, cols])
        h = (_silu(g) * u).astype(BF16)
        d = _dot(h, wd_ref[cols, :])
        if c == 0:
            acc_ref[...] = d
        else:
            acc_ref[...] += d
    o_ref[...] = x + 0.5 * _rms(acc_ref[...], gpost_ref[...])


def _stage_and_convert(w_hbm, w_ref, stage_ref, sems):
    pieces = [(r0, c0) for r0 in range(0, w_hbm.shape[0], STAGE_ROWS)
              for c0 in range(0, w_hbm.shape[1], STAGE_COLS)]

    def copy(idx):
        r0, c0 = pieces[idx]
        slot = idx % STAGE_SLOTS
        return pltpu.make_async_copy(w_hbm.at[pl.ds(r0, STAGE_ROWS), pl.ds(c0, STAGE_COLS)],
                                     stage_ref.at[slot], sems.at[slot])

    for idx in range(min(STAGE_SLOTS, len(pieces))):
        copy(idx).start()
    for idx, (r0, c0) in enumerate(pieces):
        copy(idx).wait()
        w_ref[r0:r0 + STAGE_ROWS, c0:c0 + STAGE_COLS] = stage_ref[idx % STAGE_SLOTS].astype(BF16)
        if idx + STAGE_SLOTS < len(pieces):
            copy(idx + STAGE_SLOTS).start()


def _ffn_kernel(n_a, n_cast, own_f32, xa_ref, xb_ref, gpre_ref, wg_ref, wu_ref, wd_ref, gpost_ref, *refs):
    cast_src = refs[:n_cast]
    oa_ref, ob_ref = refs[n_cast:n_cast + 2]
    cast_dst = refs[n_cast + 2:2 * n_cast + 2]
    acc_ref = refs[2 * n_cast + 2]
    i = pl.program_id(0)

    if own_f32:
        wg_hbm, wu_hbm, wd_hbm = wg_ref, wu_ref, wd_ref
        wg_ref, wu_ref, wd_ref, stage_ref, sems = refs[2 * n_cast + 3:]

        @pl.when(i == 0)
        def _():
            for w_hbm, w_ref in ((wg_hbm, wg_ref), (wu_hbm, wu_ref), (wd_hbm, wd_ref)):
                _stage_and_convert(w_hbm, w_ref, stage_ref, sems)

    for src, dst in zip(cast_src, cast_dst):
        dst[...] = src[...].astype(BF16)

    @pl.when(i < n_a)
    def _():
        _ffn_tile(xa_ref, gpre_ref, wg_ref, wu_ref, wd_ref, gpost_ref, oa_ref, acc_ref)

    @pl.when(i >= n_a)
    def _():
        _ffn_tile(xb_ref, gpre_ref, wg_ref, wu_ref, wd_ref, gpost_ref, ob_ref, acc_ref)


def _ffn(xa, xb, g_pre, w_gate, w_up, w_down, g_post, to_cast=()):
    n_a = xa.shape[0] // FFN_ROWS
    n_b = xb.shape[0] // FFN_ROWS
    own_f32 = w_gate.dtype == F32
    assert w_up.dtype == w_gate.dtype and w_down.dtype == w_gate.dtype
    spec_a = pl.BlockSpec((FFN_ROWS, D_MODEL), lambda i: (jnp.minimum(i, n_a - 1), 0))
    spec_b = pl.BlockSpec((FFN_ROWS, D_MODEL), lambda i: (jnp.maximum(i - n_a, 0), 0))
    n_slabs = CAST_SLABS
    assert n_slabs <= n_a + n_b and all(w.shape[0] % (n_slabs * BF16_SUBLANES) == 0 for w in to_cast)
    cast_specs = [pl.BlockSpec((w.shape[0] // n_slabs, w.shape[1]),
                               lambda i: (jnp.minimum(i, n_slabs - 1), 0)) for w in to_cast]
    scratch = [pltpu.VMEM((FFN_ROWS, D_MODEL), F32)]
    if own_f32:
        weight_specs = [pl.BlockSpec(memory_space=pl.ANY)] * 3
        scratch += [pltpu.VMEM((D_MODEL, FFN_DIM), BF16), pltpu.VMEM((D_MODEL, FFN_DIM), BF16),
                    pltpu.VMEM((FFN_DIM, D_MODEL), BF16),
                    pltpu.VMEM((STAGE_SLOTS, STAGE_ROWS, STAGE_COLS), F32),
                    pltpu.SemaphoreType.DMA((STAGE_SLOTS,))]
    else:
        weight_specs = [_const_spec((D_MODEL, FFN_DIM)), _const_spec((D_MODEL, FFN_DIM)),
                        _const_spec((FFN_DIM, D_MODEL))]
    return pl.pallas_call(
        functools.partial(_ffn_kernel, n_a, len(to_cast), own_f32),
        out_shape=(jax.ShapeDtypeStruct(xa.shape, F32), jax.ShapeDtypeStruct(xb.shape, F32),
                   *[jax.ShapeDtypeStruct(w.shape, BF16) for w in to_cast]),
        grid=(n_a + n_b,),
        in_specs=[
            spec_a, spec_b,
            _const_spec((1, D_MODEL)),
            *weight_specs,
            _const_spec((1, D_MODEL)),
            *cast_specs,
        ],
        out_specs=(spec_a, spec_b, *cast_specs),
        scratch_shapes=scratch,
        compiler_params=pltpu.CompilerParams(
            dimension_semantics=("arbitrary",), vmem_limit_bytes=FFN_VMEM_LIMIT),
        name="ffn",
    )(xa, xb, g_pre, w_gate, w_up, w_down, g_post, *to_cast)


def _mem_kv_tile(m_ref, g_ref, wk_ref, wv_ref, kt_ref, vt_ref, kb_ref, vb_ref):
    mn = _rms(m_ref[...], g_ref[...]).astype(BF16)
    halves = X_HEAD_DIM // LANES
    for w_ref, t_ref, b_ref in ((wk_ref, kt_ref, kb_ref), (wv_ref, vt_ref, vb_ref)):
        x = _dot(mn, w_ref[...])
        b_ref[...] = x.astype(BF16)
        for hd in range(X_HEADS):
            for j in range(halves):
                c0 = hd * X_HEAD_DIM + j * LANES
                t_ref[pl.ds(j * X_HEADS + hd, MEM_TOKENS, stride=X_HEADS * halves), :] = x[:, c0:c0 + LANES]


def _attend(q, k, v):
    outs = []
    for hd in range(X_HEADS):
        cols = slice(hd * X_HEAD_DIM, (hd + 1) * X_HEAD_DIM)
        s = _dot_nt(q[:, cols].astype(BF16), k[:, cols]) * (X_HEAD_DIM ** -0.5)
        e = jnp.exp(s - jnp.max(s, axis=-1, keepdims=True))
        p = e / jnp.sum(e, axis=-1, keepdims=True)
        outs.append(_dot(p.astype(BF16), v[:, cols]).astype(BF16))
    return jnp.concatenate(outs, axis=-1)


def _conv_block(cbuf_ref, cw_ref, base, r0, c0):
    nrow = CONV_BLOCK_ROWS
    lanes = slice(c0, c0 + LANES)
    y = None
    for res in range(SUBLANES):
        taps = [k for k in range(CONV_K) if (base + k) % SUBLANES == res]
        ext = nrow + (SUBLANES if res else 0)
        part = None
        for k in taps:
            start = r0 + base + k - res
            term = cbuf_ref[start:start + ext, lanes] * cw_ref[k:k + 1, lanes]
            part = term if part is None else part + term
        part = part[res:res + nrow, :]
        y = part if y is None else y + part
    return y


def _mix_prompt_kernel(tiles_per_seq, dec_seq, h_ref, hprev_ref, k_ref, v_ref, sq_ref, sk_ref, sv_ref,
                       gpre_ref, win_ref, slng_ref, slnb_ref,
                       ws_ref, bs_ref, cw_ref, cb_ref, clng_ref, clnb_ref, wout_ref, gpost_ref,
                       xgpre_ref, wq_ref, wo_ref, xgpost_ref,
                       o_ref, cstate_ref, chunkv_ref, satt_ref, cbuf_ref, mixed_ref, res_ref):
    s = pl.program_id(0)
    rows = MIX_ROWS
    tile = jnp.minimum(s, pl.num_programs(0) - 2)
    last_of_seq = (tile % tiles_per_seq) == tiles_per_seq - 1

    @pl.when(s == 0)
    def _():
        cbuf_ref[0:CARRY_ROWS, :] = jnp.zeros((CARRY_ROWS, CONV_WIDTH), F32)
        mixed_ref[...] = jnp.zeros((rows, D_MODEL), BF16)

    n = _rms(h_ref[...], gpre_ref[...]).astype(BF16)
    z_conv = _dot(n, win_ref[:, 2 * SGU_WIDTH:])
    a = z_conv[:, 0:CONV_WIDTH]
    gate = z_conv[:, CONV_WIDTH:]
    cbuf_ref[CARRY_ROWS:CARRY_ROWS + rows, :] = a * _sigmoid(gate)
    z_sgu = _dot(n, win_ref[:, 0:2 * SGU_WIDTH])
    u = z_sgu[:, 0:SGU_WIDTH]
    v = _ln(z_sgu[:, SGU_WIDTH:], slng_ref[...], slnb_ref[...])
    chunkv_ref[...] = v[rows - CHUNK:, :]

    res_ref[...] = hprev_ref[...] + _rms(_dot(mixed_ref[...], wout_ref[...]), gpost_ref[...])
    q = _dot(_rms(res_ref[...], xgpre_ref[...]).astype(BF16), wq_ref[...])

    row_id = lax.broadcasted_iota(jnp.int32, (CHUNK, CHUNK), 0)
    col_id = lax.broadcasted_iota(jnp.int32, (CHUNK, CHUNK), 1)
    vb = v.astype(BF16)
    s_heads = []
    for hd in range(SGU_HEADS):
        ws = jnp.where(row_id >= col_id, ws_ref[hd], 0.0).astype(BF16)
        cols = slice(hd * SGU_HEAD_DIM, (hd + 1) * SGU_HEAD_DIM)
        s_chunks = [_dot(ws, vb[c * CHUNK:(c + 1) * CHUNK, cols]) for c in range(rows // CHUNK)]
        s_heads.append(jnp.concatenate(s_chunks, axis=0))
    bias = jnp.concatenate([bs_ref[...]] * (rows // CHUNK), axis=0)
    out_a = u * (jnp.concatenate(s_heads, axis=-1) + bias)

    att = _attend(q, k_ref[...], v_ref[...])
    o_ref[...] = res_ref[...] + _rms(_dot(att, wo_ref[...]), xgpost_ref[...])

    satt_ref[...] = _cached_attention(sq_ref[...], sk_ref, sv_ref, dec_seq)

    base = CARRY_ROWS - (CONV_K - 1)
    conv = jnp.concatenate(
        [jnp.concatenate([_conv_block(cbuf_ref, cw_ref, base, r0, c0)
                          for c0 in range(0, CONV_WIDTH, LANES)], axis=-1)
         for r0 in range(0, rows, CONV_BLOCK_ROWS)], axis=0) + cb_ref[...]
    out_b = _silu(_ln(conv, clng_ref[...], clnb_ref[...]))
    cstate_ref[...] = cbuf_ref[rows + base:rows + CARRY_ROWS, :]
    cbuf_ref[0:CARRY_ROWS, :] = jnp.where(last_of_seq, 0.0, cbuf_ref[rows:rows + CARRY_ROWS, :])
    mixed_ref[...] = jnp.concatenate([out_a.astype(BF16), out_b.astype(BF16)], axis=-1)


def _mix_prompt(h, k, v, sq, sk, sv, p, tiles_per_seq, dec_seq):
    n_tiles = h.shape[0] // MIX_ROWS
    bsz = n_tiles // tiles_per_seq
    mix_tile = lambda s: jnp.minimum(s, n_tiles - 1)
    att_tile = lambda s: jnp.maximum(s - 1, 0)
    mem = pl.BlockSpec((None, MEM_TOKENS, D_MODEL), lambda s: (att_tile(s) // tiles_per_seq, 0, 0))
    slab = sk.shape[0] // n_tiles
    assert slab * n_tiles == sk.shape[0] and (slab * dec_seq) % SUBLANES == 0
    slab_step = lambda s: jnp.minimum(s, n_tiles - 1)
    sq_spec = pl.BlockSpec((slab * dec_seq, D_MODEL), lambda s: (slab_step(s), 0))
    cache_spec = pl.BlockSpec((slab,) + sk.shape[1:], lambda s: (slab_step(s), 0, 0))
    return pl.pallas_call(
        functools.partial(_mix_prompt_kernel, tiles_per_seq, dec_seq),
        out_shape=(jax.ShapeDtypeStruct(h.shape, F32),
                   jax.ShapeDtypeStruct((bsz, CONV_K - 1, CONV_WIDTH), F32),
                   jax.ShapeDtypeStruct((bsz, CHUNK, SGU_WIDTH), F32),
                   jax.ShapeDtypeStruct(sq.shape, F32)),
        grid=(n_tiles + 1,),
        in_specs=[
            pl.BlockSpec((MIX_ROWS, D_MODEL), lambda s: (mix_tile(s), 0)),
            pl.BlockSpec((MIX_ROWS, D_MODEL), lambda s: (att_tile(s), 0)), mem, mem,
            sq_spec, cache_spec, cache_spec,
            _const_spec((1, D_MODEL)),
            _const_spec((D_MODEL, 2 * SGU_WIDTH + 2 * CONV_WIDTH)),
            _const_spec((1, SGU_WIDTH)), _const_spec((1, SGU_WIDTH)),
            _const_spec((SGU_HEADS, CHUNK, CHUNK)),
            _const_spec((CHUNK, SGU_WIDTH)),
            _const_spec((CONV_K, CONV_WIDTH)), _const_spec((1, CONV_WIDTH)),
            _const_spec((1, CONV_WIDTH)), _const_spec((1, CONV_WIDTH)),
            _const_spec((D_MODEL, D_MODEL)), _const_spec((1, D_MODEL)),
            _const_spec((1, D_MODEL)),
            _const_spec((D_MODEL, D_MODEL)), _const_spec((D_MODEL, D_MODEL)),
            _const_spec((1, D_MODEL)),
        ],
        out_specs=(pl.BlockSpec((MIX_ROWS, D_MODEL), lambda s: (att_tile(s), 0)),
                   pl.BlockSpec((None, CONV_K - 1, CONV_WIDTH),
                                lambda s: (mix_tile(s) // tiles_per_seq, 0, 0)),
                   pl.BlockSpec((None, CHUNK, SGU_WIDTH),
                                lambda s: (mix_tile(s) // tiles_per_seq, 0, 0)),
                   sq_spec),
        scratch_shapes=[pltpu.VMEM((MIX_ROWS + CARRY_ROWS, CONV_WIDTH), F32),
                        pltpu.VMEM((MIX_ROWS, D_MODEL), BF16),
                        pltpu.VMEM((MIX_ROWS, D_MODEL), F32)],
        compiler_params=pltpu.CompilerParams(
            dimension_semantics=("arbitrary",), vmem_limit_bytes=VMEM_LIMIT),
        name="mix_prompt",
    )(h, h, k, v, sq, sk, sv, p["g_pre"], p["w_in"], p["sln_g"], p["sln_b"], p["w_s"], p["b_s_full"],
      p["conv_w"], p["conv_b"], p["cln_g"], p["cln_b"], p["w_out"], p["g_post"],
      p["xg_pre"], p["w_q"], p["w_o"], p["xg_post"])


def _cached_head(c_ref, b, hd):
    halves = X_HEAD_DIM // LANES
    stride = X_HEADS * halves
    parts = [c_ref[b, pl.ds(j * X_HEADS + hd, MEM_TOKENS, stride=stride), :] for j in range(halves)]
    return jnp.concatenate(parts, axis=-1).astype(BF16)


def _cached_attention(q, k_ref, v_ref, dec_seq):
    bt = k_ref.shape[0]
    scores = []
    for b in range(bt):
        for hd in range(X_HEADS):
            qbh = q[b * dec_seq:(b + 1) * dec_seq, hd * X_HEAD_DIM:(hd + 1) * X_HEAD_DIM]
            scores.append(_dot_nt(qbh.astype(BF16), _cached_head(k_ref, b, hd)))
    sc = jnp.concatenate(scores, axis=0) * (X_HEAD_DIM ** -0.5)
    e = jnp.exp(sc - jnp.max(sc, axis=-1, keepdims=True))
    pr = e / jnp.sum(e, axis=-1, keepdims=True)
    atts = []
    for b in range(bt):
        heads = []
        for hd in range(X_HEADS):
            i0 = (b * X_HEADS + hd) * dec_seq
            heads.append(_dot(pr[i0:i0 + dec_seq, :].astype(BF16), _cached_head(v_ref, b, hd)))
        atts.append(jnp.concatenate(heads, axis=-1))
    return jnp.concatenate(atts, axis=0)


def _sample_pre_kernel(dec_seq, h_ref, st_ref, m_ref, gpre_ref, win_ref, slng_ref, slnb_ref,
                       wsx_ref, bsx_ref, cwx_ref, cb_ref, clng_ref, clnb_ref, wout_ref, gpost_ref,
                       xgpre_ref, wq_ref, memg_ref, wk_ref, wv_ref,
                       h2_ref, q_ref, cstate_ref, chunkv_ref, kt_ref, vt_ref, kb_ref, vb_ref, glu_ref):
    _mem_kv_tile(m_ref, memg_ref, wk_ref, wv_ref, kt_ref, vt_ref, kb_ref, vb_ref)

    bt = SAMPLE_BATCH_TILE
    hist = CONV_K - 1
    h = h_ref[...]
    n = _rms(h, gpre_ref[...]).astype(BF16)
    z = _dot(n, win_ref[...])
    u = z[:, 0:SGU_WIDTH]
    v = _ln(z[:, SGU_WIDTH:2 * SGU_WIDTH], slng_ref[...], slnb_ref[...])
    a = z[:, 2 * SGU_WIDTH:2 * SGU_WIDTH + CONV_WIDTH]
    gate = z[:, 2 * SGU_WIDTH + CONV_WIDTH:]
    chunkv_ref[...] = v

    v3 = v.reshape(bt, dec_seq, SGU_WIDTH)
    s = jnp.broadcast_to(bsx_ref[...][None], (bt, dec_seq, SGU_WIDTH))
    for j in range(dec_seq):
        s = s + wsx_ref[j][None] * v3[:, j:j + 1, :]
    out_a = u * s.reshape(bt * dec_seq, SGU_WIDTH)

    glu = a * _sigmoid(gate)
    glu3 = glu.reshape(bt, dec_seq, CONV_WIDTH)
    new_part = jnp.broadcast_to(cb_ref[...][None], (bt, dec_seq, CONV_WIDTH))
    for r in range(dec_seq):
        new_part = new_part + cwx_ref[hist + r][None] * glu3[:, r:r + 1, :]
    old_parts = []
    for b in range(bt):
        acc = cwx_ref[0] * st_ref[0, b:b + 1, :]
        for r in range(1, hist):
            acc = acc + cwx_ref[r] * st_ref[r, b:b + 1, :]
        old_parts.append(acc)
    conv = new_part.reshape(bt * dec_seq, CONV_WIDTH) + jnp.concatenate(old_parts, axis=0)
    out_b = _silu(_ln(conv, clng_ref[...], clnb_ref[...]))

    cstate_ref[0:hist - dec_seq, :, :] = st_ref[dec_seq:hist, :, :]
    for sl in range(CONV_WIDTH // LANES):
        glu_ref[sl] = glu[:, sl * LANES:(sl + 1) * LANES]
    for tt in range(dec_seq):
        for sl in range(CONV_WIDTH // LANES):
            cstate_ref[hist - dec_seq + tt, :, sl * LANES:(sl + 1) * LANES] = (
                glu_ref[sl, pl.ds(tt, bt, stride=dec_seq), :])

    mixed = jnp.concatenate([out_a.astype(BF16), out_b.astype(BF16)], axis=-1)
    h2 = h + _rms(_dot(mixed, wout_ref[...]), gpost_ref[...])
    h2_ref[...] = h2
    q_ref[...] = _dot(_rms(h2, xgpre_ref[...]).astype(BF16), wq_ref[...])


def _sample_pre(h, state, mem, p, dec_seq):
    n_rows = h.shape[0]
    bsz = n_rows // dec_seq
    bt = SAMPLE_BATCH_TILE
    hist = CONV_K - 1
    n_slabs = bsz // bt
    n_seqs = mem.shape[0] // MEM_TOKENS
    slab = lambda i: jnp.minimum(i, n_slabs - 1)
    seq = lambda i: jnp.minimum(i, n_seqs - 1)
    tok = pl.BlockSpec((bt * dec_seq, D_MODEL), lambda i: (slab(i), 0))
    st = pl.BlockSpec((hist, bt, CONV_WIDTH), lambda i: (0, slab(i), 0))
    tiled_rows = MEM_TOKENS * D_MODEL // LANES
    tiled = jax.ShapeDtypeStruct((n_seqs, tiled_rows, LANES), F32)
    dense = jax.ShapeDtypeStruct((n_seqs, MEM_TOKENS, D_MODEL), BF16)
    tiled_spec = pl.BlockSpec((None, tiled_rows, LANES), lambda i: (seq(i), 0, 0))
    dense_spec = pl.BlockSpec((None, MEM_TOKENS, D_MODEL), lambda i: (seq(i), 0, 0))
    return pl.pallas_call(
        functools.partial(_sample_pre_kernel, dec_seq),
        out_shape=(jax.ShapeDtypeStruct((n_rows, D_MODEL), F32),
                   jax.ShapeDtypeStruct((n_rows, D_MODEL), F32),
                   jax.ShapeDtypeStruct((hist, bsz, CONV_WIDTH), F32),
                   jax.ShapeDtypeStruct((n_rows, SGU_WIDTH), F32),
                   tiled, tiled, dense, dense),
        grid=(max(n_slabs, n_seqs),),
        in_specs=[
            tok, st, pl.BlockSpec((MEM_TOKENS, D_MODEL), lambda i: (seq(i), 0)),
            _const_spec((1, D_MODEL)),
            _const_spec((D_MODEL, 2 * SGU_WIDTH + 2 * CONV_WIDTH)),
            _const_spec((1, SGU_WIDTH)), _const_spec((1, SGU_WIDTH)),
            _const_spec((dec_seq, dec_seq, SGU_WIDTH)),
            _const_spec((dec_seq, SGU_WIDTH)),
            _const_spec((hist + dec_seq, dec_seq, CONV_WIDTH)), _const_spec((1, CONV_WIDTH)),
            _const_spec((1, CONV_WIDTH)), _const_spec((1, CONV_WIDTH)),
            _const_spec((D_MODEL, D_MODEL)), _const_spec((1, D_MODEL)),
            _const_spec((1, D_MODEL)),
            _const_spec((D_MODEL, D_MODEL)),
            _const_spec((1, D_MODEL)),
            _const_spec((D_MODEL, D_MODEL)), _const_spec((D_MODEL, D_MODEL)),
        ],
        out_specs=(tok, tok, st, pl.BlockSpec((bt * dec_seq, SGU_WIDTH), lambda i: (slab(i), 0)),
                   tiled_spec, tiled_spec, dense_spec, dense_spec),
        scratch_shapes=[pltpu.VMEM((CONV_WIDTH // LANES, bt * dec_seq, LANES), F32)],
        compiler_params=pltpu.CompilerParams(
            dimension_semantics=("arbitrary",), vmem_limit_bytes=VMEM_LIMIT),
        name="sample_pre",
    )(h, state, mem, p["g_pre"], p["w_in"], p["sln_g"], p["sln_b"], p["w_s_rows"], p["b_s_rows"],
      p["conv_w_rows"], p["conv_b"], p["cln_g"], p["cln_b"], p["w_out"], p["g_post"],
      p["xg_pre"], p["w_q"], p["mem_g"], p["w_k"], p["w_v"])


def _sample_post_kernel(h2_ref, att_ref, wo_ref, g_ref, o_ref):
    o_ref[...] = h2_ref[...] + _rms(_dot(att_ref[...].astype(BF16), wo_ref[...]), g_ref[...])


def _sample_post(h2, att, p):
    rows = min(h2.shape[0], MIX_ROWS)
    tok = pl.BlockSpec((rows, D_MODEL), lambda i: (i, 0))
    return pl.pallas_call(
        _sample_post_kernel,
        out_shape=jax.ShapeDtypeStruct(h2.shape, F32),
        grid=(h2.shape[0] // rows,),
        in_specs=[tok, tok, _const_spec((D_MODEL, D_MODEL)), _const_spec((1, D_MODEL))],
        out_specs=tok,
        compiler_params=pltpu.CompilerParams(
            dimension_semantics=("arbitrary",), vmem_limit_bytes=VMEM_LIMIT),
        name="sample_post",
    )(h2, att, p["w_o"], p["xg_post"])


def _head_tiled_rows(c):
    b, m, hh, dh = c.shape
    c = c.reshape(b, m, hh, dh // LANES, LANES)
    return jnp.transpose(c, (0, 1, 3, 2, 4)).reshape(b, m * hh * (dh // LANES), LANES)


def _from_head_tiled_rows(t):
    b = t.shape[0]
    halves = X_HEAD_DIM // LANES
    t = t.reshape(b, MEM_TOKENS, halves, X_HEADS, LANES)
    return jnp.transpose(t, (0, 1, 3, 2, 4)).reshape(b, MEM_TOKENS, X_HEADS, X_HEAD_DIM)


def _row(x):
    return x.reshape(1, -1)


def kernel(x_prompt, x_sample, mem_prompt, cache_mem_k, cache_mem_v, state_conv, ffn1_g_pre, ffn1_w_gate, ffn1_w_up, ffn1_w_down, ffn1_g_post, mix_g_pre, mix_w_in, sgu_ln_g, sgu_ln_b, sgu_w_s, sgu_b_s, conv_w, conv_b, conv_ln_g, conv_ln_b, mix_w_out, mix_g_post, mem_g, xattn_g_pre, xattn_w_q, xattn_w_k, xattn_w_v, xattn_w_o, xattn_g_post, ffn2_g_pre, ffn2_w_gate, ffn2_w_up, ffn2_w_down, ffn2_g_post):
    depth = ffn1_g_pre.shape[0]
    bsz, seq, _ = x_prompt.shape
    dbsz, dec_seq, _ = x_sample.shape
    hist = CONV_K - 1
    assert seq % MIX_ROWS == 0 and dec_seq <= CHUNK and dec_seq <= hist

    hp = x_prompt.reshape(bsz * seq, D_MODEL)
    hs = x_sample.reshape(dbsz * dec_seq, D_MODEL)
    mem = mem_prompt.reshape(bsz * MEM_TOKENS, D_MODEL)
    outs = [[] for _ in range(6)]
    for l in range(depth):
        ffn1 = (_row(ffn1_g_pre[l]), ffn1_w_gate[l], ffn1_w_up[l], ffn1_w_down[l], _row(ffn1_g_post[l]))
        w_s, b_s, cw = sgu_w_s[l], sgu_b_s[l], conv_w[l]
        tril = jnp.tril(w_s[:, :dec_seq, :dec_seq])
        w_s_rows = jnp.repeat(jnp.transpose(tril, (2, 1, 0)), SGU_HEAD_DIM, axis=2)
        b_s_rows = jnp.repeat(b_s[:, :dec_seq].T, SGU_HEAD_DIM, axis=1)
        tap = jnp.arange(hist + dec_seq)[:, None] - jnp.arange(dec_seq)[None, :]
        conv_w_rows = jnp.where(((tap >= 0) & (tap < CONV_K))[:, :, None],
                                cw[jnp.clip(tap, 0, CONV_K - 1)], 0.0)
        p = dict(
            g_pre=_row(mix_g_pre[l]),
            sln_g=_row(sgu_ln_g[l]), sln_b=_row(sgu_ln_b[l]),
            w_s=w_s, b_s_full=jnp.repeat(b_s.T, SGU_HEAD_DIM, axis=1),
            w_s_rows=w_s_rows, b_s_rows=b_s_rows,
            conv_w=cw, conv_w_rows=conv_w_rows, conv_b=_row(conv_b[l]),
            cln_g=_row(conv_ln_g[l]), cln_b=_row(conv_ln_b[l]),
            g_post=_row(mix_g_post[l]), xg_pre=_row(xattn_g_pre[l]), xg_post=_row(xattn_g_post[l]),
            mem_g=_row(mem_g[l]),
        )

        later = (ffn2_w_gate[l], ffn2_w_up[l], ffn2_w_down[l],
                 mix_w_in[l], mix_w_out[l], xattn_w_q[l], xattn_w_o[l], xattn_w_k[l], xattn_w_v[l])
        (hp, hs, w2_gate, w2_up, w2_down,
         p["w_in"], p["w_out"], p["w_q"], p["w_o"], p["w_k"], p["w_v"]) = _ffn(hp, hs, *ffn1, to_cast=later)
        hs, q_s, cs_s, cv_s, k_t, v_t, k_b, v_b = _sample_pre(
            hs, jnp.transpose(state_conv[l], (1, 0, 2)), mem, p, dec_seq)
        hp, cs_p, cv_p, att_s = _mix_prompt(hp, k_b, v_b, q_s, _head_tiled_rows(cache_mem_k[l]),
                                            _head_tiled_rows(cache_mem_v[l]), p, seq // MIX_ROWS, dec_seq)
        hs = _sample_post(hs, att_s, p)
        hp, hs = _ffn(hp, hs, _row(ffn2_g_pre[l]), w2_gate, w2_up, w2_down, _row(ffn2_g_post[l]))

        outs[0].append(_from_head_tiled_rows(k_t))
        outs[1].append(_from_head_tiled_rows(v_t))
        outs[2].append(cs_p)
        outs[3].append(jnp.transpose(cs_s, (1, 0, 2)))
        outs[4].append(cv_p)
        outs[5].append(cv_s.reshape(dbsz, dec_seq, SGU_WIDTH))

    return (hp.reshape(bsz, seq, D_MODEL), hs.reshape(dbsz, dec_seq, D_MODEL),
            jnp.stack(outs[0]), jnp.stack(outs[1]), jnp.stack(outs[2]), jnp.stack(outs[3]),
            jnp.stack(outs[4]), jnp.stack(outs[5]))
```

```python
import functools

import jax
import jax.numpy as jnp
from jax import lax
from jax.experimental import pallas as pl
from jax.experimental.pallas import tpu as pltpu

D_MODEL = 1024
SGU_WIDTH = 512
CONV_WIDTH = 512
SGU_HEADS = 4
SGU_HEAD_DIM = 128
CHUNK = 128
CONV_K = 31
MEM_TOKENS = 256
X_HEADS = 4
X_HEAD_DIM = 256
FFN_DIM = 4096
EPS = 1e-6
LANES = 128
SUBLANES = 8

F32 = jnp.float32
BF16 = jnp.bfloat16

FFN_ROWS = 512
FFN_COLS = 1024
MIX_ROWS = 512
SAMPLE_BATCH_TILE = 16
CARRY_ROWS = 32
CONV_BLOCK_ROWS = 128
VMEM_LIMIT = 56 * 1024 * 1024
FFN_VMEM_LIMIT = 60 * 1024 * 1024
CAST_SLABS = 32
BF16_SUBLANES = 16
STAGE_ROWS = 256
STAGE_COLS = 1024
STAGE_SLOTS = 4


def _rms(x, g):
    return x * lax.rsqrt(jnp.mean(x * x, axis=-1, keepdims=True) + EPS) * g


def _ln(x, g, b):
    mu = jnp.mean(x, axis=-1, keepdims=True)
    xc = x - mu
    return xc * lax.rsqrt(jnp.mean(xc * xc, axis=-1, keepdims=True) + EPS) * g + b


def _sigmoid(x):
    return 0.5 + 0.5 * jnp.tanh(0.5 * x)


def _silu(x):
    half = 0.5 * x
    return half + half * jnp.tanh(half)


def _dot(a, b):
    return jnp.dot(a, b, preferred_element_type=F32)


def _dot_nt(a, b):
    return lax.dot_general(a, b, (((1,), (1,)), ((), ())), preferred_element_type=F32)


def _const_spec(shape):
    nd = len(shape)
    return pl.BlockSpec(shape, lambda *_: (0,) * nd, pipeline_mode=pl.Buffered(1))


def _ffn_tile(x_ref, gpre_ref, wg_ref, wu_ref, wd_ref, gpost_ref, o_ref, acc_ref):
    x = x_ref[...]
    n = _rms(x, gpre_ref[...]).astype(BF16)
    for c in range(FFN_DIM // FFN_COLS):
        cols = slice(c * FFN_COLS, (c + 1) * FFN_COLS)
        g = _dot(n, wg_ref[:, cols])
        u = _dot(n, wu_ref[:, cols])
        h = (_silu(g) * u).astype(BF16)
        d = _dot(h, wd_ref[cols, :])
        if c == 0:
            acc_ref[...] = d
        else:
            acc_ref[...] += d
    o_ref[...] = x + 0.5 * _rms(acc_ref[...], gpost_ref[...])


def _stage_and_convert(w_hbm, w_ref, stage_ref, sems):
    pieces = [(r0, c0) for r0 in range(0, w_hbm.shape[0], STAGE_ROWS)
              for c0 in range(0, w_hbm.shape[1], STAGE_COLS)]

    def copy(idx):
        r0, c0 = pieces[idx]
        slot = idx % STAGE_SLOTS
        return pltpu.make_async_copy(w_hbm.at[pl.ds(r0, STAGE_ROWS), pl.ds(c0, STAGE_COLS)],
                                     stage_ref.at[slot], sems.at[slot])

    for idx in range(min(STAGE_SLOTS, len(pieces))):
        copy(idx).start()
    for idx, (r0, c0) in enumerate(pieces):
        copy(idx).wait()
        w_ref[r0:r0 + STAGE_ROWS, c0:c0 + STAGE_COLS] = stage_ref[idx % STAGE_SLOTS].astype(BF16)
        if idx + STAGE_SLOTS < len(pieces):
            copy(idx + STAGE_SLOTS).start()


def _ffn_kernel(n_a, n_cast, own_f32, xa_ref, xb_ref, gpre_ref, wg_ref, wu_ref, wd_ref, gpost_ref, *refs):
    cast_src = refs[:n_cast]
    oa_ref, ob_ref = refs[n_cast:n_cast + 2]
    cast_dst = refs[n_cast + 2:2 * n_cast + 2]
    acc_ref = refs[2 * n_cast + 2]
    i = pl.program_id(0)

    if own_f32:
        wg_hbm, wu_hbm, wd_hbm = wg_ref, wu_ref, wd_ref
        wg_ref, wu_ref, wd_ref, stage_ref, sems = refs[2 * n_cast + 3:]

        @pl.when(i == 0)
        def _():
            for w_hbm, w_ref in ((wg_hbm, wg_ref), (wu_hbm, wu_ref), (wd_hbm, wd_ref)):
                _stage_and_convert(w_hbm, w_ref, stage_ref, sems)

    for src, dst in zip(cast_src, cast_dst):
        dst[...] = src[...].astype(BF16)

    @pl.when(i < n_a)
    def _():
        _ffn_tile(xa_ref, gpre_ref, wg_ref, wu_ref, wd_ref, gpost_ref, oa_ref, acc_ref)

    @pl.when(i >= n_a)
    def _():
        _ffn_tile(xb_ref, gpre_ref, wg_ref, wu_ref, wd_ref, gpost_ref, ob_ref, acc_ref)


def _ffn(xa, xb, g_pre, w_gate, w_up, w_down, g_post, to_cast=()):
    n_a = xa.shape[0] // FFN_ROWS
    n_b = xb.shape[0] // FFN_ROWS
    own_f32 = w_gate.dtype == F32
    assert w_up.dtype == w_gate.dtype and w_down.dtype == w_gate.dtype
    spec_a = pl.BlockSpec((FFN_ROWS, D_MODEL), lambda i: (jnp.minimum(i, n_a - 1), 0))
    spec_b = pl.BlockSpec((FFN_ROWS, D_MODEL), lambda i: (jnp.maximum(i - n_a, 0), 0))
    n_slabs = CAST_SLABS
    assert n_slabs <= n_a + n_b and all(w.shape[0] % (n_slabs * BF16_SUBLANES) == 0 for w in to_cast)
    cast_specs = [pl.BlockSpec((w.shape[0] // n_slabs, w.shape[1]),
                               lambda i: (jnp.minimum(i, n_slabs - 1), 0)) for w in to_cast]
    scratch = [pltpu.VMEM((FFN_ROWS, D_MODEL), F32)]
    if own_f32:
        weight_specs = [pl.BlockSpec(memory_space=pl.ANY)] * 3
        scratch += [pltpu.VMEM((D_MODEL, FFN_DIM), BF16), pltpu.VMEM((D_MODEL, FFN_DIM), BF16),
                    pltpu.VMEM((FFN_DIM, D_MODEL), BF16),
                    pltpu.VMEM((STAGE_SLOTS, STAGE_ROWS, STAGE_COLS), F32),
                    pltpu.SemaphoreType.DMA((STAGE_SLOTS,))]
    else:
        weight_specs = [_const_spec((D_MODEL, FFN_DIM)), _const_spec((D_MODEL, FFN_DIM)),
                        _const_spec((FFN_DIM, D_MODEL))]
    return pl.pallas_call(
        functools.partial(_ffn_kernel, n_a, len(to_cast), own_f32),
        out_shape=(jax.ShapeDtypeStruct(xa.shape, F32), jax.ShapeDtypeStruct(xb.shape, F32),
                   *[jax.ShapeDtypeStruct(w.shape, BF16) for w in to_cast]),
        grid=(n_a + n_b,),
        in_specs=[
            spec_a, spec_b,
            _const_spec((1, D_MODEL)),
            *weight_specs,
            _const_spec((1, D_MODEL)),
            *cast_specs,
        ],
        out_specs=(spec_a, spec_b, *cast_specs),
        scratch_shapes=scratch,
        compiler_params=pltpu.CompilerParams(
            dimension_semantics=("arbitrary",), vmem_limit_bytes=FFN_VMEM_LIMIT),
        name="ffn",
    )(xa, xb, g_pre, w_gate, w_up, w_down, g_post, *to_cast)


def _mem_kv_tile(m_ref, g_ref, wk_ref, wv_ref, kt_ref, vt_ref, kb_ref, vb_ref):
    mn = _rms(m_ref[...], g_ref[...]).astype(BF16)
    halves = X_HEAD_DIM // LANES
    for w_ref, t_ref, b_ref in ((wk_ref, kt_ref, kb_ref), (wv_ref, vt_ref, vb_ref)):
        x = _dot(mn, w_ref[...])
        b_ref[...] = x.astype(BF16)
        for hd in range(X_HEADS):
            for j in range(halves):
                c0 = hd * X_HEAD_DIM + j * LANES
                t_ref[pl.ds(j * X_HEADS + hd, MEM_TOKENS, stride=X_HEADS * halves), :] = x[:, c0:c0 + LANES]


def _attend(q, k, v):
    outs = []
    for hd in range(X_HEADS):
        cols = slice(hd * X_HEAD_DIM, (hd + 1) * X_HEAD_DIM)
        s = _dot_nt(q[:, cols].astype(BF16), k[:, cols]) * (X_HEAD_DIM ** -0.5)
        e = jnp.exp(s - jnp.max(s, axis=-1, keepdims=True))
        p = e / jnp.sum(e, axis=-1, keepdims=True)
        outs.append(_dot(p.astype(BF16), v[:, cols]).astype(BF16))
    return jnp.concatenate(outs, axis=-1)


def _conv_block(cbuf_ref, cw_ref, base, r0, c0):
    nrow = CONV_BLOCK_ROWS
    lanes = slice(c0, c0 + LANES)
    y = None
    for res in range(SUBLANES):
        taps = [k for k in range(CONV_K) if (base + k) % SUBLANES == res]
        ext = nrow + (SUBLANES if res else 0)
        part = None
        for k in taps:
            start = r0 + base + k - res
            term = cbuf_ref[start:start + ext, lanes] * cw_ref[k:k + 1, lanes]
            part = term if part is None else part + term
        part = part[res:res + nrow, :]
        y = part if y is None else y + part
    return y


def _mix_prompt_kernel(tiles_per_seq, dec_seq, h_ref, hprev_ref, k_ref, v_ref, sq_ref, sk_ref, sv_ref,
                       gpre_ref, win_ref, slng_ref, slnb_ref,
                       ws_ref, bs_ref, cw_ref, cb_ref, clng_ref, clnb_ref, wout_ref, gpost_ref,
                       xgpre_ref, wq_ref, wo_ref, xgpost_ref,
                       o_ref, cstate_ref, chunkv_ref, satt_ref, cbuf_ref, mixed_ref, res_ref):
    s = pl.program_id(0)
    last = pl.num_programs(0) - 1
    rows = MIX_ROWS

    def step(run_a, run_b):
        if run_a:
            n = _rms(h_ref[...], gpre_ref[...]).astype(BF16)
            z_conv = _dot(n, win_ref[:, 2 * SGU_WIDTH:])
            a = z_conv[:, 0:CONV_WIDTH]
            gate = z_conv[:, CONV_WIDTH:]
            cbuf_ref[CARRY_ROWS:CARRY_ROWS + rows, :] = a * _sigmoid(gate)
            z_sgu = _dot(n, win_ref[:, 0:2 * SGU_WIDTH])
            u = z_sgu[:, 0:SGU_WIDTH]
            v = _ln(z_sgu[:, SGU_WIDTH:], slng_ref[...], slnb_ref[...])
            chunkv_ref[...] = v[rows - CHUNK:, :]

        if run_b:
            res_ref[...] = hprev_ref[...] + _rms(_dot(mixed_ref[...], wout_ref[...]), gpost_ref[...])
            q = _dot(_rms(res_ref[...], xgpre_ref[...]).astype(BF16), wq_ref[...])

        if run_a:
            row_id = lax.broadcasted_iota(jnp.int32, (CHUNK, CHUNK), 0)
            col_id = lax.broadcasted_iota(jnp.int32, (CHUNK, CHUNK), 1)
            vb = v.astype(BF16)
            s_heads = []
            for hd in range(SGU_HEADS):
                ws = jnp.where(row_id >= col_id, ws_ref[hd], 0.0).astype(BF16)
                cols = slice(hd * SGU_HEAD_DIM, (hd + 1) * SGU_HEAD_DIM)
                s_chunks = [_dot(ws, vb[c * CHUNK:(c + 1) * CHUNK, cols]) for c in range(rows // CHUNK)]
                s_heads.append(jnp.concatenate(s_chunks, axis=0))
            bias = jnp.concatenate([bs_ref[...]] * (rows // CHUNK), axis=0)
            out_a = u * (jnp.concatenate(s_heads, axis=-1) + bias)

        if run_b:
            att = _attend(q, k_ref[...], v_ref[...])
            o_ref[...] = res_ref[...] + _rms(_dot(att, wo_ref[...]), xgpost_ref[...])

        if run_a:
            satt_ref[...] = _cached_attention(sq_ref[...], sk_ref, sv_ref, dec_seq)

            base = CARRY_ROWS - (CONV_K - 1)
            conv = jnp.concatenate(
                [jnp.concatenate([_conv_block(cbuf_ref, cw_ref, base, r0, c0)
                                  for c0 in range(0, CONV_WIDTH, LANES)], axis=-1)
                 for r0 in range(0, rows, CONV_BLOCK_ROWS)], axis=0) + cb_ref[...]
            out_b = _silu(_ln(conv, clng_ref[...], clnb_ref[...]))
            cstate_ref[...] = cbuf_ref[rows + base:rows + CARRY_ROWS, :]
            ends_seq = (s % tiles_per_seq) == tiles_per_seq - 1
            cbuf_ref[0:CARRY_ROWS, :] = jnp.where(ends_seq, 0.0, cbuf_ref[rows:rows + CARRY_ROWS, :])
            mixed_ref[...] = jnp.concatenate([out_a.astype(BF16), out_b.astype(BF16)], axis=-1)

    @pl.when(s == 0)
    def _():
        cbuf_ref[0:CARRY_ROWS, :] = jnp.zeros((CARRY_ROWS, CONV_WIDTH), F32)
        step(True, False)

    @pl.when((s > 0) & (s < last))
    def _():
        step(True, True)

    @pl.when(s == last)
    def _():
        step(False, True)


def _mix_prompt(h, k, v, sq, sk, sv, p, tiles_per_seq, dec_seq):
    n_tiles = h.shape[0] // MIX_ROWS
    bsz = n_tiles // tiles_per_seq
    mix_tile = lambda s: jnp.minimum(s, n_tiles - 1)
    att_tile = lambda s: jnp.maximum(s - 1, 0)
    mem = pl.BlockSpec((None, MEM_TOKENS, D_MODEL), lambda s: (att_tile(s) // tiles_per_seq, 0, 0))
    slab = sk.shape[0] // n_tiles
    assert slab * n_tiles == sk.shape[0] and (slab * dec_seq) % SUBLANES == 0
    slab_step = lambda s: jnp.minimum(s, n_tiles - 1)
    sq_spec = pl.BlockSpec((slab * dec_seq, D_MODEL), lambda s: (slab_step(s), 0))
    cache_spec = pl.BlockSpec((slab,) + sk.shape[1:], lambda s: (slab_step(s), 0, 0))
    return pl.pallas_call(
        functools.partial(_mix_prompt_kernel, tiles_per_seq, dec_seq),
        out_shape=(jax.ShapeDtypeStruct(h.shape, F32),
                   jax.ShapeDtypeStruct((bsz, CONV_K - 1, CONV_WIDTH), F32),
                   jax.ShapeDtypeStruct((bsz, CHUNK, SGU_WIDTH), F32),
                   jax.ShapeDtypeStruct(sq.shape, F32)),
        grid=(n_tiles + 1,),
        in_specs=[
            pl.BlockSpec((MIX_ROWS, D_MODEL), lambda s: (mix_tile(s), 0)),
            pl.BlockSpec((MIX_ROWS, D_MODEL), lambda s: (att_tile(s), 0)), mem, mem,
            sq_spec, cache_spec, cache_spec,
            _const_spec((1, D_MODEL)),
            _const_spec((D_MODEL, 2 * SGU_WIDTH + 2 * CONV_WIDTH)),
            _const_spec((1, SGU_WIDTH)), _const_spec((1, SGU_WIDTH)),
            _const_spec((SGU_HEADS, CHUNK, CHUNK)),
            _const_spec((CHUNK, SGU_WIDTH)),
            _const_spec((CONV_K, CONV_WIDTH)), _const_spec((1, CONV_WIDTH)),
            _const_spec((1, CONV_WIDTH)), _const_spec((1, CONV_WIDTH)),
            _const_spec((D_MODEL, D_MODEL)), _const_spec((1, D_MODEL)),
            _const_spec((1, D_MODEL)),
            _const_spec((D_MODEL, D_MODEL)), _const_spec((D_MODEL, D_MODEL)),
            _const_spec((1, D_MODEL)),
        ],
        out_specs=(pl.BlockSpec((MIX_ROWS, D_MODEL), lambda s: (att_tile(s), 0)),
                   pl.BlockSpec((None, CONV_K - 1, CONV_WIDTH),
                                lambda s: (mix_tile(s) // tiles_per_seq, 0, 0)),
                   pl.BlockSpec((None, CHUNK, SGU_WIDTH),
                                lambda s: (mix_tile(s) // tiles_per_seq, 0, 0)),
                   sq_spec),
        scratch_shapes=[pltpu.VMEM((MIX_ROWS + CARRY_ROWS, CONV_WIDTH), F32),
                        pltpu.VMEM((MIX_ROWS, D_MODEL), BF16),
                        pltpu.VMEM((MIX_ROWS, D_MODEL), F32)],
        compiler_params=pltpu.CompilerParams(
            dimension_semantics=("arbitrary",), vmem_limit_bytes=VMEM_LIMIT),
        name="mix_prompt",
    )(h, h, k, v, sq, sk, sv, p["g_pre"], p["w_in"], p["sln_g"], p["sln_b"], p["w_s"], p["b_s_full"],
      p["conv_w"], p["conv_b"], p["cln_g"], p["cln_b"], p["w_out"], p["g_post"],
      p["xg_pre"], p["w_q"], p["w_o"], p["xg_post"])


def _cached_head(c_ref, b, hd):
    halves = X_HEAD_DIM // LANES
    stride = X_HEADS * halves
    parts = [c_ref[b, pl.ds(j * X_HEADS + hd, MEM_TOKENS, stride=stride), :] for j in range(halves)]
    return jnp.concatenate(parts, axis=-1).astype(BF16)


def _cached_attention(q, k_ref, v_ref, dec_seq):
    bt = k_ref.shape[0]
    scores = []
    for b in range(bt):
        for hd in range(X_HEADS):
            qbh = q[b * dec_seq:(b + 1) * dec_seq, hd * X_HEAD_DIM:(hd + 1) * X_HEAD_DIM]
            scores.append(_dot_nt(qbh.astype(BF16), _cached_head(k_ref, b, hd)))
    sc = jnp.concatenate(scores, axis=0) * (X_HEAD_DIM ** -0.5)
    e = jnp.exp(sc - jnp.max(sc, axis=-1, keepdims=True))
    pr = e / jnp.sum(e, axis=-1, keepdims=True)
    atts = []
    for b in range(bt):
        heads = []
        for hd in range(X_HEADS):
            i0 = (b * X_HEADS + hd) * dec_seq
            heads.append(_dot(pr[i0:i0 + dec_seq, :].astype(BF16), _cached_head(v_ref, b, hd)))
        atts.append(jnp.concatenate(heads, axis=-1))
    return jnp.concatenate(atts, axis=0)


def _sample_pre_kernel(dec_seq, h_ref, st_ref, m_ref, gpre_ref, win_ref, slng_ref, slnb_ref,
                       wsx_ref, bsx_ref, cwx_ref, cb_ref, clng_ref, clnb_ref, wout_ref, gpost_ref,
                       xgpre_ref, wq_ref, memg_ref, wk_ref, wv_ref,
                       h2_ref, q_ref, cstate_ref, chunkv_ref, kt_ref, vt_ref, kb_ref, vb_ref, glu_ref):
    _mem_kv_tile(m_ref, memg_ref, wk_ref, wv_ref, kt_ref, vt_ref, kb_ref, vb_ref)

    bt = SAMPLE_BATCH_TILE
    hist = CONV_K - 1
    h = h_ref[...]
    n = _rms(h, gpre_ref[...]).astype(BF16)
    z = _dot(n, win_ref[...])
    u = z[:, 0:SGU_WIDTH]
    v = _ln(z[:, SGU_WIDTH:2 * SGU_WIDTH], slng_ref[...], slnb_ref[...])
    a = z[:, 2 * SGU_WIDTH:2 * SGU_WIDTH + CONV_WIDTH]
    gate = z[:, 2 * SGU_WIDTH + CONV_WIDTH:]
    chunkv_ref[...] = v

    v3 = v.reshape(bt, dec_seq, SGU_WIDTH)
    s = jnp.broadcast_to(bsx_ref[...][None], (bt, dec_seq, SGU_WIDTH))
    for j in range(dec_seq):
        s = s + wsx_ref[j][None] * v3[:, j:j + 1, :]
    out_a = u * s.reshape(bt * dec_seq, SGU_WIDTH)

    glu = a * _sigmoid(gate)
    glu3 = glu.reshape(bt, dec_seq, CONV_WIDTH)
    new_part = jnp.broadcast_to(cb_ref[...][None], (bt, dec_seq, CONV_WIDTH))
    for r in range(dec_seq):
        new_part = new_part + cwx_ref[hist + r][None] * glu3[:, r:r + 1, :]
    old_parts = []
    for b in range(bt):
        acc = cwx_ref[0] * st_ref[0, b:b + 1, :]
        for r in range(1, hist):
            acc = acc + cwx_ref[r] * st_ref[r, b:b + 1, :]
        old_parts.append(acc)
    conv = new_part.reshape(bt * dec_seq, CONV_WIDTH) + jnp.concatenate(old_parts, axis=0)
    out_b = _silu(_ln(conv, clng_ref[...], clnb_ref[...]))

    cstate_ref[0:hist - dec_seq, :, :] = st_ref[dec_seq:hist, :, :]
    for sl in range(CONV_WIDTH // LANES):
        glu_ref[sl] = glu[:, sl * LANES:(sl + 1) * LANES]
    for tt in range(dec_seq):
        for sl in range(CONV_WIDTH // LANES):
            cstate_ref[hist - dec_seq + tt, :, sl * LANES:(sl + 1) * LANES] = (
                glu_ref[sl, pl.ds(tt, bt, stride=dec_seq), :])

    mixed = jnp.concatenate([out_a.astype(BF16), out_b.astype(BF16)], axis=-1)
    h2 = h + _rms(_dot(mixed, wout_ref[...]), gpost_ref[...])
    h2_ref[...] = h2
    q_ref[...] = _dot(_rms(h2, xgpre_ref[...]).astype(BF16), wq_ref[...])


def _sample_pre(h, state, mem, p, dec_seq):
    n_rows = h.shape[0]
    bsz = n_rows // dec_seq
    bt = SAMPLE_BATCH_TILE
    hist = CONV_K - 1
    n_slabs = bsz // bt
    n_seqs = mem.shape[0] // MEM_TOKENS
    slab = lambda i: jnp.minimum(i, n_slabs - 1)
    seq = lambda i: jnp.minimum(i, n_seqs - 1)
    tok = pl.BlockSpec((bt * dec_seq, D_MODEL), lambda i: (slab(i), 0))
    st = pl.BlockSpec((hist, bt, CONV_WIDTH), lambda i: (0, slab(i), 0))
    tiled_rows = MEM_TOKENS * D_MODEL // LANES
    tiled = jax.ShapeDtypeStruct((n_seqs, tiled_rows, LANES), F32)
    dense = jax.ShapeDtypeStruct((n_seqs, MEM_TOKENS, D_MODEL), BF16)
    tiled_spec = pl.BlockSpec((None, tiled_rows, LANES), lambda i: (seq(i), 0, 0))
    dense_spec = pl.BlockSpec((None, MEM_TOKENS, D_MODEL), lambda i: (seq(i), 0, 0))
    return pl.pallas_call(
        functools.partial(_sample_pre_kernel, dec_seq),
        out_shape=(jax.ShapeDtypeStruct((n_rows, D_MODEL), F32),
                   jax.ShapeDtypeStruct((n_rows, D_MODEL), F32),
                   jax.ShapeDtypeStruct((hist, bsz, CONV_WIDTH), F32),
                   jax.ShapeDtypeStruct((n_rows, SGU_WIDTH), F32),
                   tiled, tiled, dense, dense),
        grid=(max(n_slabs, n_seqs),),
        in_specs=[
            tok, st, pl.BlockSpec((MEM_TOKENS, D_MODEL), lambda i: (seq(i), 0)),
            _const_spec((1, D_MODEL)),
            _const_spec((D_MODEL, 2 * SGU_WIDTH + 2 * CONV_WIDTH)),
            _const_spec((1, SGU_WIDTH)), _const_spec((1, SGU_WIDTH)),
            _const_spec((dec_seq, dec_seq, SGU_WIDTH)),
            _const_spec((dec_seq, SGU_WIDTH)),
            _const_spec((hist + dec_seq, dec_seq, CONV_WIDTH)), _const_spec((1, CONV_WIDTH)),
            _const_spec((1, CONV_WIDTH)), _const_spec((1, CONV_WIDTH)),
            _const_spec((D_MODEL, D_MODEL)), _const_spec((1, D_MODEL)),
            _const_spec((1, D_MODEL)),
            _const_spec((D_MODEL, D_MODEL)),
            _const_spec((1, D_MODEL)),
            _const_spec((D_MODEL, D_MODEL)), _const_spec((D_MODEL, D_MODEL)),
        ],
        out_specs=(tok, tok, st, pl.BlockSpec((bt * dec_seq, SGU_WIDTH), lambda i: (slab(i), 0)),
                   tiled_spec, tiled_spec, dense_spec, dense_spec),
        scratch_shapes=[pltpu.VMEM((CONV_WIDTH // LANES, bt * dec_seq, LANES), F32)],
        compiler_params=pltpu.CompilerParams(
            dimension_semantics=("arbitrary",), vmem_limit_bytes=VMEM_LIMIT),
        name="sample_pre",
    )(h, state, mem, p["g_pre"], p["w_in"], p["sln_g"], p["sln_b"], p["w_s_rows"], p["b_s_rows"],
      p["conv_w_rows"], p["conv_b"], p["cln_g"], p["cln_b"], p["w_out"], p["g_post"],
      p["xg_pre"], p["w_q"], p["mem_g"], p["w_k"], p["w_v"])


def _sample_post_kernel(h2_ref, att_ref, wo_ref, g_ref, o_ref):
    o_ref[...] = h2_ref[...] + _rms(_dot(att_ref[...].astype(BF16), wo_ref[...]), g_ref[...])


def _sample_post(h2, att, p):
    rows = min(h2.shape[0], MIX_ROWS)
    tok = pl.BlockSpec((rows, D_MODEL), lambda i: (i, 0))
    return pl.pallas_call(
        _sample_post_kernel,
        out_shape=jax.ShapeDtypeStruct(h2.shape, F32),
        grid=(h2.shape[0] // rows,),
        in_specs=[tok, tok, _const_spec((D_MODEL, D_MODEL)), _const_spec((1, D_MODEL))],
        out_specs=tok,
        compiler_params=pltpu.CompilerParams(
            dimension_semantics=("arbitrary",), vmem_limit_bytes=VMEM_LIMIT),
        name="sample_post",
    )(h2, att, p["w_o"], p["xg_post"])


def _head_tiled_rows(c):
    b, m, hh, dh = c.shape
    c = c.reshape(b, m, hh, dh // LANES, LANES)
    return jnp.transpose(c, (0, 1, 3, 2, 4)).reshape(b, m * hh * (dh // LANES), LANES)


def _from_head_tiled_rows(t):
    b = t.shape[0]
    halves = X_HEAD_DIM // LANES
    t = t.reshape(b, MEM_TOKENS, halves, X_HEADS, LANES)
    return jnp.transpose(t, (0, 1, 3, 2, 4)).reshape(b, MEM_TOKENS, X_HEADS, X_HEAD_DIM)


def _row(x):
    return x.reshape(1, -1)


def kernel(x_prompt, x_sample, mem_prompt, cache_mem_k, cache_mem_v, state_conv, ffn1_g_pre, ffn1_w_gate, ffn1_w_up, ffn1_w_down, ffn1_g_post, mix_g_pre, mix_w_in, sgu_ln_g, sgu_ln_b, sgu_w_s, sgu_b_s, conv_w, conv_b, conv_ln_g, conv_ln_b, mix_w_out, mix_g_post, mem_g, xattn_g_pre, xattn_w_q, xattn_w_k, xattn_w_v, xattn_w_o, xattn_g_post, ffn2_g_pre, ffn2_w_gate, ffn2_w_up, ffn2_w_down, ffn2_g_post):
    depth = ffn1_g_pre.shape[0]
    bsz, seq, _ = x_prompt.shape
    dbsz, dec_seq, _ = x_sample.shape
    hist = CONV_K - 1
    assert seq % MIX_ROWS == 0 and dec_seq <= CHUNK and dec_seq <= hist

    hp = x_prompt.reshape(bsz * seq, D_MODEL)
    hs = x_sample.reshape(dbsz * dec_seq, D_MODEL)
    mem = mem_prompt.reshape(bsz * MEM_TOKENS, D_MODEL)
    outs = [[] for _ in range(6)]
    for l in range(depth):
        ffn1 = (_row(ffn1_g_pre[l]), ffn1_w_gate[l], ffn1_w_up[l], ffn1_w_down[l], _row(ffn1_g_post[l]))
        w_s, b_s, cw = sgu_w_s[l], sgu_b_s[l], conv_w[l]
        tril = jnp.tril(w_s[:, :dec_seq, :dec_seq])
        w_s_rows = jnp.repeat(jnp.transpose(tril, (2, 1, 0)), SGU_HEAD_DIM, axis=2)
        b_s_rows = jnp.repeat(b_s[:, :dec_seq].T, SGU_HEAD_DIM, axis=1)
        tap = jnp.arange(hist + dec_seq)[:, None] - jnp.arange(dec_seq)[None, :]
        conv_w_rows = jnp.where(((tap >= 0) & (tap < CONV_K))[:, :, None],
                                cw[jnp.clip(tap, 0, CONV_K - 1)], 0.0)
        p = dict(
            g_pre=_row(mix_g_pre[l]),
            sln_g=_row(sgu_ln_g[l]), sln_b=_row(sgu_ln_b[l]),
            w_s=w_s, b_s_full=jnp.repeat(b_s.T, SGU_HEAD_DIM, axis=1),
            w_s_rows=w_s_rows, b_s_rows=b_s_rows,
            conv_w=cw, conv_w_rows=conv_w_rows, conv_b=_row(conv_b[l]),
            cln_g=_row(conv_ln_g[l]), cln_b=_row(conv_ln_b[l]),
            g_post=_row(mix_g_post[l]), xg_pre=_row(xattn_g_pre[l]), xg_post=_row(xattn_g_post[l]),
            mem_g=_row(mem_g[l]),
        )

        later = (ffn2_w_gate[l], ffn2_w_up[l], ffn2_w_down[l],
                 mix_w_in[l], mix_w_out[l], xattn_w_q[l], xattn_w_o[l], xattn_w_k[l], xattn_w_v[l])
        (hp, hs, w2_gate, w2_up, w2_down,
         p["w_in"], p["w_out"], p["w_q"], p["w_o"], p["w_k"], p["w_v"]) = _ffn(hp, hs, *ffn1, to_cast=later)
        hs, q_s, cs_s, cv_s, k_t, v_t, k_b, v_b = _sample_pre(
            hs, jnp.transpose(state_conv[l], (1, 0, 2)), mem, p, dec_seq)
        hp, cs_p, cv_p, att_s = _mix_prompt(hp, k_b, v_b, q_s, _head_tiled_rows(cache_mem_k[l]),
                                            _head_tiled_rows(cache_mem_v[l]), p, seq // MIX_ROWS, dec_seq)
        hs = _sample_post(hs, att_s, p)
        hp, hs = _ffn(hp, hs, _row(ffn2_g_pre[l]), w2_gate, w2_up, w2_down, _row(ffn2_g_post[l]))

        outs[0].append(_from_head_tiled_rows(k_t))
        outs[1].append(_from_head_tiled_rows(v_t))
        outs[2].append(cs_p)
        outs[3].append(jnp.transpose(cs_s, (1, 0, 2)))
        outs[4].append(cv_p)
        outs[5].append(cv_s.reshape(dbsz, dec_seq, SGU_WIDTH))

    return (hp.reshape(bsz, seq, D_MODEL), hs.reshape(dbsz, dec_seq, D_MODEL),
            jnp.stack(outs[0]), jnp.stack(outs[1]), jnp.stack(outs[2]), jnp.stack(outs[3]),
            jnp.stack(outs[4]), jnp.stack(outs[5]))
```

```python
import functools

import jax
import jax.numpy as jnp
from jax import lax
from jax.experimental import pallas as pl
from jax.experimental.pallas import tpu as pltpu

D_MODEL = 1024
SGU_WIDTH = 512
CONV_WIDTH = 512
SGU_HEADS = 4
SGU_HEAD_DIM = 128
CHUNK = 128
CONV_K = 31
MEM_TOKENS = 256
X_HEADS = 4
X_HEAD_DIM = 256
FFN_DIM = 4096
EPS = 1e-6
LANES = 128
SUBLANES = 8

F32 = jnp.float32
BF16 = jnp.bfloat16

FFN_ROWS = 512
FFN_COLS = 1024
MIX_ROWS = 512
SAMPLE_BATCH_TILE = 16
CARRY_ROWS = 32
CONV_BLOCK_ROWS = 128
CONV_SLICES = 5
VMEM_LIMIT = 56 * 1024 * 1024
FFN_VMEM_LIMIT = 60 * 1024 * 1024
CAST_SLABS = 32
BF16_SUBLANES = 16
STAGE_ROWS = 256
STAGE_COLS = 1024
STAGE_SLOTS = 4


def _rms(x, g):
    return x * lax.rsqrt(jnp.mean(x * x, axis=-1, keepdims=True) + EPS) * g


def _ln(x, g, b):
    mu = jnp.mean(x, axis=-1, keepdims=True)
    xc = x - mu
    return xc * lax.rsqrt(jnp.mean(xc * xc, axis=-1, keepdims=True) + EPS) * g + b


def _sigmoid(x):
    return 0.5 + 0.5 * jnp.tanh(0.5 * x)


def _silu(x):
    half = 0.5 * x
    return half + half * jnp.tanh(half)


def _dot(a, b):
    return jnp.dot(a, b, preferred_element_type=F32)


def _dot_nt(a, b):
    return lax.dot_general(a, b, (((1,), (1,)), ((), ())), preferred_element_type=F32)


def _const_spec(shape):
    nd = len(shape)
    return pl.BlockSpec(shape, lambda *_: (0,) * nd, pipeline_mode=pl.Buffered(1))


def _ffn_tile(x_ref, gpre_ref, wg_ref, wu_ref, wd_ref, gpost_ref, o_ref, acc_ref):
    x = x_ref[...]
    n = _rms(x, gpre_ref[...]).astype(BF16)
    for c in range(FFN_DIM // FFN_COLS):
        cols = slice(c * FFN_COLS, (c + 1) * FFN_COLS)
        g = _dot(n, wg_ref[:, cols])
        u = _dot(n, wu_ref[:, cols])
        h = (_silu(g) * u).astype(BF16)
        d = _dot(h, wd_ref[cols, :])
        if c == 0:
            acc_ref[...] = d
        else:
            acc_ref[...] += d
    o_ref[...] = x + 0.5 * _rms(acc_ref[...], gpost_ref[...])


def _stage_and_convert(w_hbm, w_ref, stage_ref, sems):
    pieces = [(r0, c0) for r0 in range(0, w_hbm.shape[0], STAGE_ROWS)
              for c0 in range(0, w_hbm.shape[1], STAGE_COLS)]

    def copy(idx):
        r0, c0 = pieces[idx]
        slot = idx % STAGE_SLOTS
        return pltpu.make_async_copy(w_hbm.at[pl.ds(r0, STAGE_ROWS), pl.ds(c0, STAGE_COLS)],
                                     stage_ref.at[slot], sems.at[slot])

    for idx in range(min(STAGE_SLOTS, len(pieces))):
        copy(idx).start()
    for idx, (r0, c0) in enumerate(pieces):
        copy(idx).wait()
        w_ref[r0:r0 + STAGE_ROWS, c0:c0 + STAGE_COLS] = stage_ref[idx % STAGE_SLOTS].astype(BF16)
        if idx + STAGE_SLOTS < len(pieces):
            copy(idx + STAGE_SLOTS).start()


def _ffn_kernel(n_a, n_cast, own_f32, xa_ref, xb_ref, gpre_ref, wg_ref, wu_ref, wd_ref, gpost_ref, *refs):
    cast_src = refs[:n_cast]
    oa_ref, ob_ref = refs[n_cast:n_cast + 2]
    cast_dst = refs[n_cast + 2:2 * n_cast + 2]
    acc_ref = refs[2 * n_cast + 2]
    i = pl.program_id(0)

    if own_f32:
        wg_hbm, wu_hbm, wd_hbm = wg_ref, wu_ref, wd_ref
        wg_ref, wu_ref, wd_ref, stage_ref, sems = refs[2 * n_cast + 3:]

        @pl.when(i == 0)
        def _():
            for w_hbm, w_ref in ((wg_hbm, wg_ref), (wu_hbm, wu_ref), (wd_hbm, wd_ref)):
                _stage_and_convert(w_hbm, w_ref, stage_ref, sems)

    for src, dst in zip(cast_src, cast_dst):
        dst[...] = src[...].astype(BF16)

    @pl.when(i < n_a)
    def _():
        _ffn_tile(xa_ref, gpre_ref, wg_ref, wu_ref, wd_ref, gpost_ref, oa_ref, acc_ref)

    @pl.when(i >= n_a)
    def _():
        _ffn_tile(xb_ref, gpre_ref, wg_ref, wu_ref, wd_ref, gpost_ref, ob_ref, acc_ref)


def _ffn(xa, xb, g_pre, w_gate, w_up, w_down, g_post, to_cast=()):
    n_a = xa.shape[0] // FFN_ROWS
    n_b = xb.shape[0] // FFN_ROWS
    own_f32 = w_gate.dtype == F32
    assert w_up.dtype == w_gate.dtype and w_down.dtype == w_gate.dtype
    spec_a = pl.BlockSpec((FFN_ROWS, D_MODEL), lambda i: (jnp.minimum(i, n_a - 1), 0))
    spec_b = pl.BlockSpec((FFN_ROWS, D_MODEL), lambda i: (jnp.maximum(i - n_a, 0), 0))
    n_slabs = CAST_SLABS
    assert n_slabs <= n_a + n_b and all(w.shape[0] % (n_slabs * BF16_SUBLANES) == 0 for w in to_cast)
    cast_specs = [pl.BlockSpec((w.shape[0] // n_slabs, w.shape[1]),
                               lambda i: (jnp.minimum(i, n_slabs - 1), 0)) for w in to_cast]
    scratch = [pltpu.VMEM((FFN_ROWS, D_MODEL), F32)]
    if own_f32:
        weight_specs = [pl.BlockSpec(memory_space=pl.ANY)] * 3
        scratch += [pltpu.VMEM((D_MODEL, FFN_DIM), BF16), pltpu.VMEM((D_MODEL, FFN_DIM), BF16),
                    pltpu.VMEM((FFN_DIM, D_MODEL), BF16),
                    pltpu.VMEM((STAGE_SLOTS, STAGE_ROWS, STAGE_COLS), F32),
                    pltpu.SemaphoreType.DMA((STAGE_SLOTS,))]
    else:
        weight_specs = [_const_spec((D_MODEL, FFN_DIM)), _const_spec((D_MODEL, FFN_DIM)),
                        _const_spec((FFN_DIM, D_MODEL))]
    return pl.pallas_call(
        functools.partial(_ffn_kernel, n_a, len(to_cast), own_f32),
        out_shape=(jax.ShapeDtypeStruct(xa.shape, F32), jax.ShapeDtypeStruct(xb.shape, F32),
                   *[jax.ShapeDtypeStruct(w.shape, BF16) for w in to_cast]),
        grid=(n_a + n_b,),
        in_specs=[
            spec_a, spec_b,
            _const_spec((1, D_MODEL)),
            *weight_specs,
            _const_spec((1, D_MODEL)),
            *cast_specs,
        ],
        out_specs=(spec_a, spec_b, *cast_specs),
        scratch_shapes=scratch,
        compiler_params=pltpu.CompilerParams(
            dimension_semantics=("arbitrary",), vmem_limit_bytes=FFN_VMEM_LIMIT),
        name="ffn",
    )(xa, xb, g_pre, w_gate, w_up, w_down, g_post, *to_cast)


def _mem_kv_tile(m_ref, g_ref, wk_ref, wv_ref, kt_ref, vt_ref, kb_ref, vb_ref):
    mn = _rms(m_ref[...], g_ref[...]).astype(BF16)
    halves = X_HEAD_DIM // LANES
    for w_ref, t_ref, b_ref in ((wk_ref, kt_ref, kb_ref), (wv_ref, vt_ref, vb_ref)):
        x = _dot(mn, w_ref[...])
        b_ref[...] = x.astype(BF16)
        for hd in range(X_HEADS):
            for j in range(halves):
                c0 = hd * X_HEAD_DIM + j * LANES
                t_ref[pl.ds(j * X_HEADS + hd, MEM_TOKENS, stride=X_HEADS * halves), :] = x[:, c0:c0 + LANES]


def _attend(q, k, v):
    outs = []
    for hd in range(X_HEADS):
        cols = slice(hd * X_HEAD_DIM, (hd + 1) * X_HEAD_DIM)
        s = _dot_nt(q[:, cols].astype(BF16), k[:, cols]) * (X_HEAD_DIM ** -0.5)
        e = jnp.exp(s - jnp.max(s, axis=-1, keepdims=True))
        p = e / jnp.sum(e, axis=-1, keepdims=True)
        outs.append(_dot(p.astype(BF16), v[:, cols]).astype(BF16))
    return jnp.concatenate(outs, axis=-1)


def _conv_block(cbuf_ref, cw_ref, base, r0, c0):
    nrow = CONV_BLOCK_ROWS
    lanes = slice(c0, c0 + LANES)
    y = None
    for res in range(SUBLANES):
        taps = [k for k in range(CONV_K) if (base + k) % SUBLANES == res]
        ext = nrow + (SUBLANES if res else 0)
        part = None
        for k in taps:
            start = r0 + base + k - res
            term = cbuf_ref[start:start + ext, lanes] * cw_ref[k:k + 1, lanes]
            part = term if part is None else part + term
        part = part[res:res + nrow, :]
        y = part if y is None else y + part
    return y


def _after(ref, rows, dep, nomask_ref):
    bits = pltpu.bitcast(dep[0:rows, 0:LANES], jnp.uint32)
    zero = pltpu.bitcast(bits & nomask_ref[0:rows, :], F32)
    ref[0:rows, 0:LANES] = ref[0:rows, 0:LANES] + zero.astype(ref.dtype)


def _mix_prompt_kernel(tiles_per_seq, dec_seq, hx_ref, hz_ref, k_ref, v_ref, sq_ref, sk_ref, sv_ref,
                       nomask_ref, gpre_ref, win_ref, slng_ref, slnb_ref,
                       ws_ref, bs_ref, cw_ref, cb_ref, clng_ref, clnb_ref, wout_ref, gpost_ref,
                       xgpre_ref, wq_ref, wo_ref, xgpost_ref,
                       o_ref, cstate_ref, chunkv_ref, satt_ref,
                       cbuf_ref, glu_ref, outa0_ref, outa1_ref, outa2_ref, outb0_ref, outb1_ref,
                       n_ref, att_ref, res_ref):
    s = pl.program_id(0)
    rows = MIX_ROWS
    n_tiles = pl.num_programs(0) - 2
    tile_y = jnp.clip(s - 1, 0, n_tiles - 1)
    y_ends_seq = (tile_y % tiles_per_seq) == tiles_per_seq - 1

    @pl.when(s == 0)
    def _():
        cbuf_ref[0:CARRY_ROWS, :] = jnp.zeros((CARRY_ROWS, CONV_WIDTH), F32)
        glu_ref[...] = jnp.zeros((rows, CONV_WIDTH), F32)
        for ref in (outa0_ref, outa1_ref, outb0_ref):
            ref[...] = jnp.zeros((rows, SGU_WIDTH), BF16)

    cbuf_ref[CARRY_ROWS:CARRY_ROWS + rows, :] = glu_ref[...]
    outa2_ref[...] = outa1_ref[...]
    outa1_ref[...] = outa0_ref[...]
    outb1_ref[...] = outb0_ref[...]

    base = CARRY_ROWS - (CONV_K - 1)
    conv_blocks = [(r0, c0) for r0 in range(0, rows, CONV_BLOCK_ROWS)
                   for c0 in range(0, CONV_WIDTH, LANES)]
    conv_parts = {}

    def conv_slice(count):
        fold = None
        for r0, c0 in conv_blocks[len(conv_parts):len(conv_parts) + count]:
            block = _conv_block(cbuf_ref, cw_ref, base, r0, c0)
            conv_parts[(r0, c0)] = block
            for g in range(0, CONV_BLOCK_ROWS, BF16_SUBLANES):
                piece = block[g:g + BF16_SUBLANES, :]
                fold = piece if fold is None else fold + piece
        return fold

    per_slice = len(conv_blocks) // CONV_SLICES

    n_ref[...] = _rms(hx_ref[...], gpre_ref[...]).astype(BF16)
    ready = conv_slice(per_slice)
    ahead = conv_slice(per_slice)
    _after(n_ref, BF16_SUBLANES, ready, nomask_ref)
    z_conv = _dot(n_ref[...], win_ref[:, 2 * SGU_WIDTH:])
    glu_ref[...] = z_conv[:, 0:CONV_WIDTH] * _sigmoid(z_conv[:, CONV_WIDTH:])
    ready, ahead = ahead, conv_slice(per_slice)
    _after(n_ref, BF16_SUBLANES, ready, nomask_ref)
    z_sgu = _dot(n_ref[...], win_ref[:, 0:2 * SGU_WIDTH])
    u = z_sgu[:, 0:SGU_WIDTH]
    v = _ln(z_sgu[:, SGU_WIDTH:], slng_ref[...], slnb_ref[...])
    chunkv_ref[...] = v[rows - CHUNK:, :]

    ready, ahead = ahead, conv_slice(per_slice)
    _after(outa2_ref, BF16_SUBLANES, ready, nomask_ref)
    mixed = jnp.concatenate([outa2_ref[...], outb1_ref[...]], axis=-1)
    res_ref[...] = hz_ref[...] + _rms(_dot(mixed, wout_ref[...]), gpost_ref[...])
    ready, ahead = ahead, conv_slice(len(conv_blocks))
    _after(res_ref, SUBLANES, ready, nomask_ref)
    q = _dot(_rms(res_ref[...], xgpre_ref[...]).astype(BF16), wq_ref[...])

    row_id = lax.broadcasted_iota(jnp.int32, (CHUNK, CHUNK), 0)
    col_id = lax.broadcasted_iota(jnp.int32, (CHUNK, CHUNK), 1)
    vb = v.astype(BF16)
    s_heads = []
    for hd in range(SGU_HEADS):
        ws = jnp.where(row_id >= col_id, ws_ref[hd], 0.0).astype(BF16)
        cols = slice(hd * SGU_HEAD_DIM, (hd + 1) * SGU_HEAD_DIM)
        s_chunks = [_dot(ws, vb[c * CHUNK:(c + 1) * CHUNK, cols]) for c in range(rows // CHUNK)]
        s_heads.append(jnp.concatenate(s_chunks, axis=0))
    bias = jnp.concatenate([bs_ref[...]] * (rows // CHUNK), axis=0)
    outa0_ref[...] = (u * (jnp.concatenate(s_heads, axis=-1) + bias)).astype(BF16)

    att_ref[...] = _attend(q, k_ref[...], v_ref[...])
    _after(att_ref, BF16_SUBLANES, ahead, nomask_ref)
    o_ref[...] = res_ref[...] + _rms(_dot(att_ref[...], wo_ref[...]), xgpost_ref[...])

    satt_ref[...] = _cached_attention(sq_ref[...], sk_ref, sv_ref, dec_seq)

    conv = jnp.concatenate(
        [jnp.concatenate([conv_parts[(r0, c0)] for c0 in range(0, CONV_WIDTH, LANES)], axis=-1)
         for r0 in range(0, rows, CONV_BLOCK_ROWS)], axis=0) + cb_ref[...]
    outb0_ref[...] = _silu(_ln(conv, clng_ref[...], clnb_ref[...])).astype(BF16)
    cstate_ref[...] = cbuf_ref[rows + base:rows + CARRY_ROWS, :]
    cbuf_ref[0:CARRY_ROWS, :] = jnp.where(y_ends_seq, 0.0, cbuf_ref[rows:rows + CARRY_ROWS, :])


def _mix_prompt(h, k, v, sq, sk, sv, p, tiles_per_seq, dec_seq):
    n_tiles = h.shape[0] // MIX_ROWS
    bsz = n_tiles // tiles_per_seq
    tile_x = lambda s: jnp.minimum(s, n_tiles - 1)
    tile_y = lambda s: jnp.clip(s - 1, 0, n_tiles - 1)
    tile_z = lambda s: jnp.clip(s - 2, 0, n_tiles - 1)
    mem = pl.BlockSpec((None, MEM_TOKENS, D_MODEL), lambda s: (tile_z(s) // tiles_per_seq, 0, 0))
    slab = sk.shape[0] // n_tiles
    assert slab * n_tiles == sk.shape[0] and (slab * dec_seq) % SUBLANES == 0
    sq_spec = pl.BlockSpec((slab * dec_seq, D_MODEL), lambda s: (tile_x(s), 0))
    cache_spec = pl.BlockSpec((slab,) + sk.shape[1:], lambda s: (tile_x(s), 0, 0))
    half = pltpu.VMEM((MIX_ROWS, SGU_WIDTH), BF16)
    return pl.pallas_call(
        functools.partial(_mix_prompt_kernel, tiles_per_seq, dec_seq),
        out_shape=(jax.ShapeDtypeStruct(h.shape, F32),
                   jax.ShapeDtypeStruct((bsz, CONV_K - 1, CONV_WIDTH), F32),
                   jax.ShapeDtypeStruct((bsz, CHUNK, SGU_WIDTH), F32),
                   jax.ShapeDtypeStruct(sq.shape, F32)),
        grid=(n_tiles + 2,),
        in_specs=[
            pl.BlockSpec((MIX_ROWS, D_MODEL), lambda s: (tile_x(s), 0)),
            pl.BlockSpec((MIX_ROWS, D_MODEL), lambda s: (tile_z(s), 0)), mem, mem,
            sq_spec, cache_spec, cache_spec,
            _const_spec((BF16_SUBLANES, LANES)),
            _const_spec((1, D_MODEL)),
            _const_spec((D_MODEL, 2 * SGU_WIDTH + 2 * CONV_WIDTH)),
            _const_spec((1, SGU_WIDTH)), _const_spec((1, SGU_WIDTH)),
            _const_spec((SGU_HEADS, CHUNK, CHUNK)),
            _const_spec((CHUNK, SGU_WIDTH)),
            _const_spec((CONV_K, CONV_WIDTH)), _const_spec((1, CONV_WIDTH)),
            _const_spec((1, CONV_WIDTH)), _const_spec((1, CONV_WIDTH)),
            _const_spec((D_MODEL, D_MODEL)), _const_spec((1, D_MODEL)),
            _const_spec((1, D_MODEL)),
            _const_spec((D_MODEL, D_MODEL)), _const_spec((D_MODEL, D_MODEL)),
            _const_spec((1, D_MODEL)),
        ],
        out_specs=(pl.BlockSpec((MIX_ROWS, D_MODEL), lambda s: (tile_z(s), 0)),
                   pl.BlockSpec((None, CONV_K - 1, CONV_WIDTH),
                                lambda s: (tile_y(s) // tiles_per_seq, 0, 0)),
                   pl.BlockSpec((None, CHUNK, SGU_WIDTH),
                                lambda s: (tile_x(s) // tiles_per_seq, 0, 0)),
                   sq_spec),
        scratch_shapes=[pltpu.VMEM((MIX_ROWS + CARRY_ROWS, CONV_WIDTH), F32),
                        pltpu.VMEM((MIX_ROWS, CONV_WIDTH), F32),
                        half, half, half, half, half,
                        pltpu.VMEM((MIX_ROWS, D_MODEL), BF16),
                        pltpu.VMEM((MIX_ROWS, D_MODEL), BF16),
                        pltpu.VMEM((MIX_ROWS, D_MODEL), F32)],
        compiler_params=pltpu.CompilerParams(
            dimension_semantics=("arbitrary",), vmem_limit_bytes=FFN_VMEM_LIMIT),
        name="mix_prompt",
    )(h, h, k, v, sq, sk, sv, jnp.zeros((BF16_SUBLANES, LANES), jnp.uint32), p["g_pre"], p["w_in"], p["sln_g"], p["sln_b"], p["w_s"], p["b_s_full"],
      p["conv_w"], p["conv_b"], p["cln_g"], p["cln_b"], p["w_out"], p["g_post"],
      p["xg_pre"], p["w_q"], p["w_o"], p["xg_post"])


def _cached_head(c_ref, b, hd):
    halves = X_HEAD_DIM // LANES
    stride = X_HEADS * halves
    parts = [c_ref[b, pl.ds(j * X_HEADS + hd, MEM_TOKENS, stride=stride), :] for j in range(halves)]
    return jnp.concatenate(parts, axis=-1).astype(BF16)


def _cached_attention(q, k_ref, v_ref, dec_seq):
    bt = k_ref.shape[0]
    scores = []
    for b in range(bt):
        for hd in range(X_HEADS):
            qbh = q[b * dec_seq:(b + 1) * dec_seq, hd * X_HEAD_DIM:(hd + 1) * X_HEAD_DIM]
            scores.append(_dot_nt(qbh.astype(BF16), _cached_head(k_ref, b, hd)))
    sc = jnp.concatenate(scores, axis=0) * (X_HEAD_DIM ** -0.5)
    e = jnp.exp(sc - jnp.max(sc, axis=-1, keepdims=True))
    pr = e / jnp.sum(e, axis=-1, keepdims=True)
    atts = []
    for b in range(bt):
        heads = []
        for hd in range(X_HEADS):
            i0 = (b * X_HEADS + hd) * dec_seq
            heads.append(_dot(pr[i0:i0 + dec_seq, :].astype(BF16), _cached_head(v_ref, b, hd)))
        atts.append(jnp.concatenate(heads, axis=-1))
    return jnp.concatenate(atts, axis=0)


def _sample_pre_kernel(dec_seq, h_ref, st_ref, m_ref, gpre_ref, win_ref, slng_ref, slnb_ref,
                       wsx_ref, bsx_ref, cwx_ref, cb_ref, clng_ref, clnb_ref, wout_ref, gpost_ref,
                       xgpre_ref, wq_ref, memg_ref, wk_ref, wv_ref,
                       h2_ref, q_ref, cstate_ref, chunkv_ref, kt_ref, vt_ref, kb_ref, vb_ref, glu_ref):
    _mem_kv_tile(m_ref, memg_ref, wk_ref, wv_ref, kt_ref, vt_ref, kb_ref, vb_ref)

    bt = SAMPLE_BATCH_TILE
    hist = CONV_K - 1
    h = h_ref[...]
    n = _rms(h, gpre_ref[...]).astype(BF16)
    z = _dot(n, win_ref[...])
    u = z[:, 0:SGU_WIDTH]
    v = _ln(z[:, SGU_WIDTH:2 * SGU_WIDTH], slng_ref[...], slnb_ref[...])
    a = z[:, 2 * SGU_WIDTH:2 * SGU_WIDTH + CONV_WIDTH]
    gate = z[:, 2 * SGU_WIDTH + CONV_WIDTH:]
    chunkv_ref[...] = v

    v3 = v.reshape(bt, dec_seq, SGU_WIDTH)
    s = jnp.broadcast_to(bsx_ref[...][None], (bt, dec_seq, SGU_WIDTH))
    for j in range(dec_seq):
        s = s + wsx_ref[j][None] * v3[:, j:j + 1, :]
    out_a = u * s.reshape(bt * dec_seq, SGU_WIDTH)

    glu = a * _sigmoid(gate)
    glu3 = glu.reshape(bt, dec_seq, CONV_WIDTH)
    new_part = jnp.broadcast_to(cb_ref[...][None], (bt, dec_seq, CONV_WIDTH))
    for r in range(dec_seq):
        new_part = new_part + cwx_ref[hist + r][None] * glu3[:, r:r + 1, :]
    old_parts = []
    for b in range(bt):
        acc = cwx_ref[0] * st_ref[0, b:b + 1, :]
        for r in range(1, hist):
            acc = acc + cwx_ref[r] * st_ref[r, b:b + 1, :]
        old_parts.append(acc)
    conv = new_part.reshape(bt * dec_seq, CONV_WIDTH) + jnp.concatenate(old_parts, axis=0)
    out_b = _silu(_ln(conv, clng_ref[...], clnb_ref[...]))

    cstate_ref[0:hist - dec_seq, :, :] = st_ref[dec_seq:hist, :, :]
    for sl in range(CONV_WIDTH // LANES):
        glu_ref[sl] = glu[:, sl * LANES:(sl + 1) * LANES]
    for tt in range(dec_seq):
        for sl in range(CONV_WIDTH // LANES):
            cstate_ref[hist - dec_seq + tt, :, sl * LANES:(sl + 1) * LANES] = (
                glu_ref[sl, pl.ds(tt, bt, stride=dec_seq), :])

    mixed = jnp.concatenate([out_a.astype(BF16), out_b.astype(BF16)], axis=-1)
    h2 = h + _rms(_dot(mixed, wout_ref[...]), gpost_ref[...])
    h2_ref[...] = h2
    q_ref[...] = _dot(_rms(h2, xgpre_ref[...]).astype(BF16), wq_ref[...])


def _sample_pre(h, state, mem, p, dec_seq):
    n_rows = h.shape[0]
    bsz = n_rows // dec_seq
    bt = SAMPLE_BATCH_TILE
    hist = CONV_K - 1
    n_slabs = bsz // bt
    n_seqs = mem.shape[0] // MEM_TOKENS
    slab = lambda i: jnp.minimum(i, n_slabs - 1)
    seq = lambda i: jnp.minimum(i, n_seqs - 1)
    tok = pl.BlockSpec((bt * dec_seq, D_MODEL), lambda i: (slab(i), 0))
    st = pl.BlockSpec((hist, bt, CONV_WIDTH), lambda i: (0, slab(i), 0))
    tiled_rows = MEM_TOKENS * D_MODEL // LANES
    tiled = jax.ShapeDtypeStruct((n_seqs, tiled_rows, LANES), F32)
    dense = jax.ShapeDtypeStruct((n_seqs, MEM_TOKENS, D_MODEL), BF16)
    tiled_spec = pl.BlockSpec((None, tiled_rows, LANES), lambda i: (seq(i), 0, 0))
    dense_spec = pl.BlockSpec((None, MEM_TOKENS, D_MODEL), lambda i: (seq(i), 0, 0))
    return pl.pallas_call(
        functools.partial(_sample_pre_kernel, dec_seq),
        out_shape=(jax.ShapeDtypeStruct((n_rows, D_MODEL), F32),
                   jax.ShapeDtypeStruct((n_rows, D_MODEL), F32),
                   jax.ShapeDtypeStruct((hist, bsz, CONV_WIDTH), F32),
                   jax.ShapeDtypeStruct((n_rows, SGU_WIDTH), F32),
                   tiled, tiled, dense, dense),
        grid=(max(n_slabs, n_seqs),),
        in_specs=[
            tok, st, pl.BlockSpec((MEM_TOKENS, D_MODEL), lambda i: (seq(i), 0)),
            _const_spec((1, D_MODEL)),
            _const_spec((D_MODEL, 2 * SGU_WIDTH + 2 * CONV_WIDTH)),
            _const_spec((1, SGU_WIDTH)), _const_spec((1, SGU_WIDTH)),
            _const_spec((dec_seq, dec_seq, SGU_WIDTH)),
            _const_spec((dec_seq, SGU_WIDTH)),
            _const_spec((hist + dec_seq, dec_seq, CONV_WIDTH)), _const_spec((1, CONV_WIDTH)),
            _const_spec((1, CONV_WIDTH)), _const_spec((1, CONV_WIDTH)),
            _const_spec((D_MODEL, D_MODEL)), _const_spec((1, D_MODEL)),
            _const_spec((1, D_MODEL)),
            _const_spec((D_MODEL, D_MODEL)),
            _const_spec((1, D_MODEL)),
            _const_spec((D_MODEL, D_MODEL)), _const_spec((D_MODEL, D_MODEL)),
        ],
        out_specs=(tok, tok, st, pl.BlockSpec((bt * dec_seq, SGU_WIDTH), lambda i: (slab(i), 0)),
                   tiled_spec, tiled_spec, dense_spec, dense_spec),
        scratch_shapes=[pltpu.VMEM((CONV_WIDTH // LANES, bt * dec_seq, LANES), F32)],
        compiler_params=pltpu.CompilerParams(
            dimension_semantics=("arbitrary",), vmem_limit_bytes=VMEM_LIMIT),
        name="sample_pre",
    )(h, state, mem, p["g_pre"], p["w_in"], p["sln_g"], p["sln_b"], p["w_s_rows"], p["b_s_rows"],
      p["conv_w_rows"], p["conv_b"], p["cln_g"], p["cln_b"], p["w_out"], p["g_post"],
      p["xg_pre"], p["w_q"], p["mem_g"], p["w_k"], p["w_v"])


def _sample_post_kernel(h2_ref, att_ref, wo_ref, g_ref, o_ref):
    o_ref[...] = h2_ref[...] + _rms(_dot(att_ref[...].astype(BF16), wo_ref[...]), g_ref[...])


def _sample_post(h2, att, p):
    rows = min(h2.shape[0], MIX_ROWS)
    tok = pl.BlockSpec((rows, D_MODEL), lambda i: (i, 0))
    return pl.pallas_call(
        _sample_post_kernel,
        out_shape=jax.ShapeDtypeStruct(h2.shape, F32),
        grid=(h2.shape[0] // rows,),
        in_specs=[tok, tok, _const_spec((D_MODEL, D_MODEL)), _const_spec((1, D_MODEL))],
        out_specs=tok,
        compiler_params=pltpu.CompilerParams(
            dimension_semantics=("arbitrary",), vmem_limit_bytes=VMEM_LIMIT),
        name="sample_post",
    )(h2, att, p["w_o"], p["xg_post"])


def _head_tiled_rows(c):
    b, m, hh, dh = c.shape
    c = c.reshape(b, m, hh, dh // LANES, LANES)
    return jnp.transpose(c, (0, 1, 3, 2, 4)).reshape(b, m * hh * (dh // LANES), LANES)


def _from_head_tiled_rows(t):
    b = t.shape[0]
    halves = X_HEAD_DIM // LANES
    t = t.reshape(b, MEM_TOKENS, halves, X_HEADS, LANES)
    return jnp.transpose(t, (0, 1, 3, 2, 4)).reshape(b, MEM_TOKENS, X_HEADS, X_HEAD_DIM)


def _row(x):
    return x.reshape(1, -1)


def kernel(x_prompt, x_sample, mem_prompt, cache_mem_k, cache_mem_v, state_conv, ffn1_g_pre, ffn1_w_gate, ffn1_w_up, ffn1_w_down, ffn1_g_post, mix_g_pre, mix_w_in, sgu_ln_g, sgu_ln_b, sgu_w_s, sgu_b_s, conv_w, conv_b, conv_ln_g, conv_ln_b, mix_w_out, mix_g_post, mem_g, xattn_g_pre, xattn_w_q, xattn_w_k, xattn_w_v, xattn_w_o, xattn_g_post, ffn2_g_pre, ffn2_w_gate, ffn2_w_up, ffn2_w_down, ffn2_g_post):
    depth = ffn1_g_pre.shape[0]
    bsz, seq, _ = x_prompt.shape
    dbsz, dec_seq, _ = x_sample.shape
    hist = CONV_K - 1
    assert seq % MIX_ROWS == 0 and dec_seq <= CHUNK and dec_seq <= hist

    hp = x_prompt.reshape(bsz * seq, D_MODEL)
    hs = x_sample.reshape(dbsz * dec_seq, D_MODEL)
    mem = mem_prompt.reshape(bsz * MEM_TOKENS, D_MODEL)
    outs = [[] for _ in range(6)]
    for l in range(depth):
        ffn1 = (_row(ffn1_g_pre[l]), ffn1_w_gate[l], ffn1_w_up[l], ffn1_w_down[l], _row(ffn1_g_post[l]))
        w_s, b_s, cw = sgu_w_s[l], sgu_b_s[l], conv_w[l]
        tril = jnp.tril(w_s[:, :dec_seq, :dec_seq])
        w_s_rows = jnp.repeat(jnp.transpose(tril, (2, 1, 0)), SGU_HEAD_DIM, axis=2)
        b_s_rows = jnp.repeat(b_s[:, :dec_seq].T, SGU_HEAD_DIM, axis=1)
        tap = jnp.arange(hist + dec_seq)[:, None] - jnp.arange(dec_seq)[None, :]
        conv_w_rows = jnp.where(((tap >= 0) & (tap < CONV_K))[:, :, None],
                                cw[jnp.clip(tap, 0, CONV_K - 1)], 0.0)
        p = dict(
            g_pre=_row(mix_g_pre[l]),
            sln_g=_row(sgu_ln_g[l]), sln_b=_row(sgu_ln_b[l]),
            w_s=w_s, b_s_full=jnp.repeat(b_s.T, SGU_HEAD_DIM, axis=1),
            w_s_rows=w_s_rows, b_s_rows=b_s_rows,
            conv_w=cw, conv_w_rows=conv_w_rows, conv_b=_row(conv_b[l]),
            cln_g=_row(conv_ln_g[l]), cln_b=_row(conv_ln_b[l]),
            g_post=_row(mix_g_post[l]), xg_pre=_row(xattn_g_pre[l]), xg_post=_row(xattn_g_post[l]),
            mem_g=_row(mem_g[l]),
        )

        later = (ffn2_w_gate[l], ffn2_w_up[l], ffn2_w_down[l],
                 mix_w_in[l], mix_w_out[l], xattn_w_q[l], xattn_w_o[l], xattn_w_k[l], xattn_w_v[l])
        (hp, hs, w2_gate, w2_up, w2_down,
         p["w_in"], p["w_out"], p["w_q"], p["w_o"], p["w_k"], p["w_v"]) = _ffn(hp, hs, *ffn1, to_cast=later)
        hs, q_s, cs_s, cv_s, k_t, v_t, k_b, v_b = _sample_pre(
            hs, jnp.transpose(state_conv[l], (1, 0, 2)), mem, p, dec_seq)
        hp, cs_p, cv_p, att_s = _mix_prompt(hp, k_b, v_b, q_s, _head_tiled_rows(cache_mem_k[l]),
                                            _head_tiled_rows(cache_mem_v[l]), p, seq // MIX_ROWS, dec_seq)
        hs = _sample_post(hs, att_s, p)
        hp, hs = _ffn(hp, hs, _row(ffn2_g_pre[l]), w2_gate, w2_up, w2_down, _row(ffn2_g_post[l]))

        outs[0].append(_from_head_tiled_rows(k_t))
        outs[1].append(_from_head_tiled_rows(v_t))
        outs[2].append(cs_p)
        outs[3].append(jnp.transpose(cs_s, (1, 0, 2)))
        outs[4].append(cv_p)
        outs[5].append(cv_s.reshape(dbsz, dec_seq, SGU_WIDTH))

    return (hp.reshape(bsz, seq, D_MODEL), hs.reshape(dbsz, dec_seq, D_MODEL),
            jnp.stack(outs[0]), jnp.stack(outs[1]), jnp.stack(outs[2]), jnp.stack(outs[3]),
            jnp.stack(outs[4]), jnp.stack(outs[5]))
```
